```python
import math
import jax, jax.numpy as jnp
from jax import lax
import numpy as np

D_MODEL = 1024
BATCH = 8
SEQ = 4096
DEPTH = 1

MIX_WIDTH = D_MODEL
FOURIER_WIDTH = MIX_WIDTH // 2
FOURIER_GROUPS = 8
FOURIER_GROUP_DIM = FOURIER_WIDTH // FOURIER_GROUPS
HYENA_WIDTH = MIX_WIDTH - FOURIER_WIDTH
HYENA_ORDER = 2
SHORT_CONV = 3
POS_BANDS = 16
POS_EMB_DIM = 1 + 2 * POS_BANDS
FILTER_HIDDEN = 64
FILTER_INNER_LAYERS = 2
N_DIRECTIONS = 2
IN_WIDTH = FOURIER_WIDTH + (HYENA_ORDER + 1) * HYENA_WIDTH
D_FF = 4 * D_MODEL
DECAY_TARGET = 1e-2
FAST_DECAY_PCT = 0.3
SLOW_DECAY_PCT = 1.5
MAX_DECAY = math.log(DECAY_TARGET) / FAST_DECAY_PCT
MIN_DECAY = math.log(DECAY_TARGET) / SLOW_DECAY_PCT
EPS = 1e-5

kernel_name = 'hybrid_fnet_hyena_parallel_block'


def rmsnorm(x, g):
    xf = x.astype(jnp.float32)
    y = xf * lax.rsqrt(jnp.mean(xf * xf, axis=-1, keepdims=True) + EPS)
    return (y * g.astype(jnp.float32)).astype(x.dtype)


def fourier_mix(u):
    b, l, _ = u.shape
    ug = u.astype(jnp.float32).reshape(b, l, FOURIER_GROUPS, FOURIER_GROUP_DIM)
    f = jnp.fft.fft2(ug, axes=(1, 3), norm='ortho').real
    return f.reshape(b, l, FOURIER_WIDTH).astype(u.dtype)


def short_conv(u, w, bias):
    l = u.shape[1]
    pad = SHORT_CONV // 2
    up = jnp.pad(u, ((0, 0), (pad, SHORT_CONV - 1 - pad), (0, 0)))
    y = up[:, 0:l] * w[0]
    for j in range(1, SHORT_CONV):
        y = y + up[:, j:j + l] * w[j]
    return y + bias


def positional_features(l):
    pos = jnp.arange(l, dtype=jnp.float32)
    t = pos / jnp.float32(max(l - 1, 1))
    bands = jnp.linspace(1e-4, POS_BANDS - 1, POS_BANDS, dtype=jnp.float32)
    ang = (2.0 * jnp.pi * pos / l)[:, None] * bands[None, :]
    z = jnp.concatenate([t[:, None], jnp.cos(ang), -jnp.sin(ang)], axis=-1)
    return z, t


def implicit_filters(l, w0, b0, w_inner, b_inner, freq, w_out):
    z, t = positional_features(l)
    f32 = jnp.float32
    fr = freq.astype(f32)
    h = jnp.sin(fr * (z @ w0.astype(f32) + b0.astype(f32)))
    for j in range(FILTER_INNER_LAYERS):
        h = jnp.sin(fr * (h @ w_inner[j].astype(f32) + b_inner[j].astype(f32)))
    h = (h @ w_out.astype(f32)).reshape(l, HYENA_ORDER, N_DIRECTIONS, HYENA_WIDTH)
    deltas = jnp.linspace(MIN_DECAY, MAX_DECAY, HYENA_WIDTH, dtype=f32)
    decay = jnp.exp(-t[:, None] * jnp.abs(deltas)[None, :])
    h = h * decay[:, None, None, :]
    h_fwd = h[:, :, 0]
    h_bwd = h[:, :, 1]
    k = jnp.concatenate([h_fwd[:1] + h_bwd[:1], h_fwd[1:], jnp.zeros_like(h_fwd[:1]),
                         h_bwd[1:][::-1]], axis=0)
    return k * lax.rsqrt(jnp.sum(k * k, axis=0, keepdims=True) + EPS)


def fft_long_conv(u, k_freq, d):
    l = u.shape[1]
    uf32 = u.astype(jnp.float32)
    uf = jnp.fft.rfft(uf32, n=2 * l, axis=1)
    y = jnp.fft.irfft(uf * k_freq[None], n=2 * l, axis=1)[:, :l]
    return (y + uf32 * d.astype(jnp.float32)).astype(u.dtype)


def hyena_mix(u, conv_w, conv_b, filt_w0, filt_b0, filt_w_inner, filt_b_inner, filt_freq,
              filt_w_out, long_d):
    l = u.shape[1]
    u = short_conv(u, conv_w, conv_b)
    x1, x2, v = jnp.split(u, HYENA_ORDER + 1, axis=-1)
    k = implicit_filters(l, filt_w0, filt_b0, filt_w_inner, filt_b_inner, filt_freq, filt_w_out)
    k_freq = jnp.fft.rfft(k, axis=0)
    z = v
    for n, gate in enumerate((x1, x2)):
        z = gate * fft_long_conv(z, k_freq[:, n], long_d[n])
    return z


def setup_inputs(seed: int = 0) -> dict:
    key = jax.random.key(seed)
    ks = jax.random.split(key, 20)
    f32 = jnp.float32

    def nrm(k, shape, scale):
        return jax.random.normal(k, shape, f32) * scale

    def gain(k, shape):
        return 1.0 + 0.02 * jax.random.normal(k, shape, f32)

    return {
        'x': jax.random.normal(ks[0], (BATCH, SEQ, D_MODEL), f32),
        'g_mix': gain(ks[1], (DEPTH, D_MODEL)),
        'w_in': nrm(ks[2], (DEPTH, D_MODEL, IN_WIDTH), D_MODEL ** -0.5),
        'conv_w': nrm(ks[3], (DEPTH, SHORT_CONV, (HYENA_ORDER + 1) * HYENA_WIDTH), SHORT_CONV ** -0.5),
        'conv_b': nrm(ks[4], (DEPTH, (HYENA_ORDER + 1) * HYENA_WIDTH), 0.01),
        'filt_w0': nrm(ks[5], (DEPTH, POS_EMB_DIM, FILTER_HIDDEN), POS_EMB_DIM ** -0.5),
        'filt_b0': nrm(ks[6], (DEPTH, FILTER_HIDDEN), 0.1),
        'filt_w_inner': nrm(ks[7], (DEPTH, FILTER_INNER_LAYERS, FILTER_HIDDEN, FILTER_HIDDEN), FILTER_HIDDEN ** -0.5),
        'filt_b_inner': nrm(ks[8], (DEPTH, FILTER_INNER_LAYERS, FILTER_HIDDEN), 0.1),
        'filt_freq': gain(ks[9], (DEPTH, FILTER_HIDDEN)),
        'filt_w_out': nrm(ks[10], (DEPTH, FILTER_HIDDEN, HYENA_ORDER * N_DIRECTIONS * HYENA_WIDTH), FILTER_HIDDEN ** -0.5),
        'long_d': nrm(ks[11], (DEPTH, HYENA_ORDER, HYENA_WIDTH), 0.1),
        'g_fourier': gain(ks[12], (DEPTH, FOURIER_WIDTH)),
        'g_hyena': gain(ks[13], (DEPTH, HYENA_WIDTH)),
        'w_out': nrm(ks[14], (DEPTH, MIX_WIDTH, D_MODEL), MIX_WIDTH ** -0.5),
        'g_mlp': gain(ks[15], (DEPTH, D_MODEL)),
        'w_fc1': nrm(ks[16], (DEPTH, D_MODEL, D_FF), D_MODEL ** -0.5),
        'w_fc2': nrm(ks[17], (DEPTH, D_FF, D_MODEL), D_FF ** -0.5),
        'g_final': gain(ks[18], (D_MODEL,)),
    }


def reference(x, g_mix, w_in, conv_w, conv_b, filt_w0, filt_b0, filt_w_inner, filt_b_inner,
              filt_freq, filt_w_out, long_d, g_fourier, g_hyena, w_out, g_mlp, w_fc1, w_fc2,
              g_final):
    for i in range(DEPTH):
        h = rmsnorm(x, g_mix[i])
        p = jnp.einsum('bld,de->ble', h, w_in[i])
        u_f = p[..., :FOURIER_WIDTH]
        u_h = p[..., FOURIER_WIDTH:]
        y_f = rmsnorm(fourier_mix(u_f), g_fourier[i])
        y_h = rmsnorm(hyena_mix(u_h, conv_w[i], conv_b[i], filt_w0[i], filt_b0[i], filt_w_inner[i],
                                filt_b_inner[i], filt_freq[i], filt_w_out[i], long_d[i]), g_hyena[i])
        y = jnp.concatenate([y_f, y_h], axis=-1)
        x = x + jnp.einsum('ble,ed->bld', y, w_out[i])
        hm = rmsnorm(x, g_mlp[i])
        a = jnp.square(jax.nn.relu(jnp.einsum('bld,df->blf', hm, w_fc1[i])))
        x = x + jnp.einsum('blf,fd->bld', a, w_fc2[i])
    return rmsnorm(x, g_final)
```

```python
import functools
import math

import numpy as np
import jax
import jax.numpy as jnp
from jax import lax
from jax.experimental import pallas as pl
from jax.experimental.pallas import tpu as pltpu

F32 = jnp.float32
BF16 = jnp.bfloat16

D_MODEL = 1024
BATCH = 8
SEQ = 4096
FOURIER_WIDTH = 512
FOURIER_GROUP_DIM = 64
HYENA_WIDTH = 512
HYENA_ORDER = 2
POS_BANDS = 16
POS_EMB_DIM = 1 + 2 * POS_BANDS
FILTER_HIDDEN = 64
FILTER_INNER_LAYERS = 2
D_FF = 4 * D_MODEL
DECAY_TARGET = 1e-2
MAX_DECAY = math.log(DECAY_TARGET) / 0.3
MIN_DECAY = math.log(DECAY_TARGET) / 1.5
EPS = 1e-5

LANES = 128
N2 = 128
CHUNKS = SEQ // N2
CHUNK_PITCH = N2 + 8
SEQ_ROWS = CHUNKS * CHUNK_PITCH
N1F = SEQ // N2
N1C = 2 * SEQ // N2
SLAB = 2 * N2 + 8
FPITCH = N1F + 8
FROWS = N2 * FPITCH
IN_COLS = 2 * FOURIER_WIDTH + 3 * HYENA_WIDTH
VMEM_LIMIT = 60 * 1024 * 1024


def _cs(rows, cols, n):
    m = (np.outer(np.arange(rows), np.arange(cols)) % n).astype(np.float64)
    ang = 2.0 * np.pi * m / n
    return np.cos(ang), np.sin(ang)


def _tables():
    t = {}
    c, s = _cs(FOURIER_GROUP_DIM, FOURIER_GROUP_DIM, FOURIER_GROUP_DIM)
    eye = np.eye(FOURIER_WIDTH // FOURIER_GROUP_DIM)
    scale = 1.0 / math.sqrt(SEQ * FOURIER_GROUP_DIM)
    t['chan'] = np.concatenate([np.kron(eye, c), -np.kron(eye, s)], axis=1) * scale
    c, s = _cs(N1F, N1F, N1F)
    t['m1f'] = np.block([[c, s], [-s, c]])
    c, s = _cs(N1C, CHUNKS, N1C)
    t['m1c'] = np.block([[c, s], [-s, c]])
    c, s = _cs(N1C, N1C, N1C)
    t['m1k'] = np.concatenate([c, -s], axis=0)
    c, s = _cs(CHUNKS, N1C, N1C)
    t['m1i'] = np.block([[c, -s], [s, c]])
    c, s = _cs(N2, N2, N2)
    t['m2'] = np.block([[c, s], [-s, c]])
    t['m2i'] = np.block([[c, -s], [s, c]])
    t['m2r'] = np.concatenate([c, s], axis=1)
    return {k: jnp.asarray(v.astype(np.float32)) for k, v in t.items()}


def _twiddles(n1, n):
    r = lax.broadcasted_iota(jnp.int32, (N2 * n1, LANES), 0)
    m = (r // n1) * (r % n1)
    ang = m.astype(F32) * F32(2.0 * math.pi / n)
    return jnp.cos(ang), jnp.sin(ang)


def _const_spec(shape):
    nd = len(shape)
    return pl.BlockSpec(shape, lambda *_: (0,) * nd, pipeline_mode=pl.Buffered(1))


def _params(sem):
    return pltpu.CompilerParams(dimension_semantics=sem, vmem_limit_bytes=VMEM_LIMIT)


def _prep_kernel(w_ref, g_ref, chan_ref, o_ref):
    g = g_ref[...]
    wf = w_ref[:, :FOURIER_WIDTH]
    z = jnp.dot(wf, chan_ref[...], preferred_element_type=F32,
                precision=lax.Precision.HIGHEST)
    o_ref[:, :2 * FOURIER_WIDTH] = (g * z).astype(BF16)
    o_ref[:, 2 * FOURIER_WIDTH:] = (g * w_ref[:, FOURIER_WIDTH:]).astype(BF16)


def _prep_weights(w_in, g_mix, chan):
    return pl.pallas_call(
        _prep_kernel,
        out_shape=jax.ShapeDtypeStruct((D_MODEL, IN_COLS), BF16),
        compiler_params=_params(None),
        name='prep_weights',
    )(w_in, g_mix.reshape(D_MODEL, 1), chan)


ROW_TILE = 512
TILE_CHUNKS = ROW_TILE // N2


def _inproj_kernel(x_ref, w_ref, o_ref):
    x = x_ref[0]
    r = lax.rsqrt(jnp.mean(x * x, axis=-1, keepdims=True) + EPS)
    h = (x * r).astype(BF16)
    p = jnp.dot(h, w_ref[...], preferred_element_type=F32)
    for j in range(TILE_CHUNKS):
        o_ref[0, j * CHUNK_PITCH:j * CHUNK_PITCH + N2, :] = p[j * N2:(j + 1) * N2]
        o_ref[0, j * CHUNK_PITCH + N2:(j + 1) * CHUNK_PITCH, :] = jnp.zeros(
            (CHUNK_PITCH - N2, IN_COLS), F32)


def _inproj(x, w):
    return pl.pallas_call(
        _inproj_kernel,
        grid=(BATCH, SEQ // ROW_TILE),
        in_specs=[pl.BlockSpec((1, ROW_TILE, D_MODEL), lambda b, i: (b, i, 0)),
                  _const_spec((D_MODEL, IN_COLS))],
        out_specs=pl.BlockSpec((1, TILE_CHUNKS * CHUNK_PITCH, IN_COLS), lambda b, i: (b, i, 0)),
        out_shape=jax.ShapeDtypeStruct((BATCH, SEQ_ROWS, IN_COLS), F32),
        compiler_params=_params(('parallel', 'parallel')),
        name='in_proj',
    )(x, w)


def _shortconv_kernel(p_ref, w_ref, b_ref, o_ref):
    w0 = w_ref[0:1, :]
    w1 = w_ref[1:2, :]
    w2 = w_ref[2:3, :]
    bias = b_ref[...]
    row = lax.broadcasted_iota(jnp.int32, (N2, LANES), 0)
    zero_row = jnp.zeros((1, LANES), F32)
    for n1 in range(CHUNKS):
        base = n1 * CHUNK_PITCH
        cur = p_ref[0, base:base + N2, :]
        before = p_ref[0, base - CHUNK_PITCH + N2 - 1:base - CHUNK_PITCH + N2, :] if n1 > 0 else zero_row
        after = p_ref[0, base + CHUNK_PITCH:base + CHUNK_PITCH + 1, :] if n1 < CHUNKS - 1 else zero_row
        up = jnp.where(row == 0, before, pltpu.roll(cur, 1, axis=0))
        dn = jnp.where(row == N2 - 1, after, pltpu.roll(cur, N2 - 1, axis=0))
        o_ref[0, base:base + N2, :] = up * w0 + cur * w1 + dn * w2 + bias
        o_ref[0, base + N2:base + CHUNK_PITCH, :] = jnp.zeros((CHUNK_PITCH - N2, LANES), F32)


def _shortconv(p, conv_w, conv_b):
    nblk = 3 * HYENA_WIDTH // LANES
    first = 2 * FOURIER_WIDTH // LANES
    return pl.pallas_call(
        _shortconv_kernel,
        grid=(BATCH, nblk),
        in_specs=[pl.BlockSpec((1, SEQ_ROWS, LANES), lambda b, c: (b, 0, first + c)),
                  pl.BlockSpec((3, LANES), lambda b, c: (0, c)),
                  pl.BlockSpec((1, LANES), lambda b, c: (0, c))],
        out_specs=pl.BlockSpec((1, SEQ_ROWS, LANES), lambda b, c: (b, 0, c)),
        out_shape=jax.ShapeDtypeStruct((BATCH, SEQ_ROWS, 3 * HYENA_WIDTH), F32),
        compiler_params=_params(('parallel', 'parallel')),
        name='short_conv',
    )(p, conv_w, conv_b.reshape(1, -1))


def _store_stage1(a_ref, a, n1, n2, twc_ref, tws_ref):
    for h in range(2):
        ar = a[:n1, h * LANES:(h + 1) * LANES]
        ai = a[n1:, h * LANES:(h + 1) * LANES]
        row0 = pl.multiple_of((n2 + h) * n1, n1)
        c = twc_ref[pl.ds(row0, n1), :]
        s = tws_ref[pl.ds(row0, n1), :]
        a_ref[pl.ds(n2 + h, n1, stride=SLAB), :] = ar * c + ai * s
        a_ref[pl.ds(N2 + n2 + h, n1, stride=SLAB), :] = ai * c - ar * s


def _load_slab_pair(a_ref, k1):
    off0 = pl.multiple_of(k1 * SLAB, 8)
    off1 = pl.multiple_of((k1 + 1) * SLAB, 8)
    x = jnp.concatenate([a_ref[pl.ds(off0, 2 * N2), :], a_ref[pl.ds(off1, 2 * N2), :]], axis=1)
    return x, off0, off1


def _fourier_kernel(zr_ref, zi_ref, m1_ref, m2_ref, twc_ref, tws_ref, o_ref, a_ref):
    def stage1(j, carry):
        n2 = 2 * j
        cols = []
        for h in range(2):
            cols.append(jnp.concatenate(
                [zr_ref[0, pl.ds(n2 + h, N1F, stride=CHUNK_PITCH), :],
                 zi_ref[0, pl.ds(n2 + h, N1F, stride=CHUNK_PITCH), :]], axis=0))
        x = jnp.concatenate(cols, axis=1).astype(BF16)
        a = jnp.dot(m1_ref[...], x, preferred_element_type=F32)
        _store_stage1(a_ref, a, N1F, n2, twc_ref, tws_ref)
        return carry

    lax.fori_loop(0, N2 // 2, stage1, 0)

    def stage2(j, carry):
        k1 = 2 * j
        x, _, _ = _load_slab_pair(a_ref, k1)
        y = jnp.dot(m2_ref[...], x.astype(BF16), preferred_element_type=F32)
        o_ref[0, pl.ds(k1, N2, stride=FPITCH), :] = y[:, :LANES]
        o_ref[0, pl.ds(k1 + 1, N2, stride=FPITCH), :] = y[:, LANES:]
        return carry

    lax.fori_loop(0, N1F // 2, stage2, 0)
    for i in range(FPITCH - N1F):
        o_ref[0, pl.ds(N1F + i, N2, stride=FPITCH), :] = jnp.zeros((N2, LANES), F32)


def _fourier(p, m1, m2r, twc, tws):
    nblk = FOURIER_WIDTH // LANES
    return pl.pallas_call(
        _fourier_kernel,
        grid=(BATCH, nblk),
        in_specs=[pl.BlockSpec((1, SEQ_ROWS, LANES), lambda b, c: (b, 0, c)),
                  pl.BlockSpec((1, SEQ_ROWS, LANES), lambda b, c: (b, 0, nblk + c)),
                  _const_spec(m1.shape), _const_spec(m2r.shape),
                  _const_spec(twc.shape), _const_spec(tws.shape)],
        out_specs=pl.BlockSpec((1, FROWS, LANES), lambda b, c: (b, 0, c)),
        out_shape=jax.ShapeDtypeStruct((BATCH, FROWS, FOURIER_WIDTH), F32),
        scratch_shapes=[pltpu.VMEM((N1F * SLAB, LANES), F32)],
        compiler_params=_params(('parallel', 'parallel')),
        name='fourier_fft',
    )(p, p, m1, m2r, twc, tws)


FILT_ROWS = 512


def _filter_kernel(feat_ref, w0_ref, b0_ref, wi_ref, bi_ref, fr_ref, wf_ref, wb_ref, dl_ref,
                   m1_ref, m2_ref, twc_ref, tws_ref, o_ref, h_ref, k_ref, a_ref):
    hi = lax.Precision.HIGHEST
    nchunk = 2 * SEQ // FILT_ROWS

    @pl.when(pl.program_id(0) == 0)
    def _():
        fr = fr_ref[...]

        def mlp(i, carry):
            r0 = pl.multiple_of(i * FILT_ROWS, FILT_ROWS)
            h = jnp.sin(fr * (jnp.dot(feat_ref[pl.ds(r0, FILT_ROWS), :], w0_ref[...],
                                      preferred_element_type=F32, precision=hi) + b0_ref[...]))
            for j in range(FILTER_INNER_LAYERS):
                h = jnp.sin(fr * (jnp.dot(h, wi_ref[j], preferred_element_type=F32, precision=hi)
                                  + bi_ref[j]))
            h_ref[pl.ds(r0, FILT_ROWS), :] = h
            return carry

        lax.fori_loop(0, nchunk, mlp, 0)

    def taps(i, ss):
        r0 = pl.multiple_of(i * FILT_ROWS, FILT_ROWS)
        r = r0 + lax.broadcasted_iota(jnp.int32, (FILT_ROWS, LANES), 0)
        p = (r % N1C) * N2 + r // N1C
        pos = jnp.where(p <= SEQ, p, 2 * SEQ - p)
        t = pos.astype(F32) / F32(SEQ - 1)
        decay = jnp.exp(-t * jnp.abs(dl_ref[...]))
        h = h_ref[pl.ds(r0, FILT_ROWS), :]
        kf = jnp.dot(h, wf_ref[...], preferred_element_type=F32, precision=hi)
        kb = jnp.dot(h, wb_ref[...], preferred_element_type=F32, precision=hi)
        k = jnp.where(p < SEQ, kf, jnp.where(p == SEQ, 0.0, kb)) + jnp.where(p == 0, kb, 0.0)
        k = k * decay
        k_ref[pl.ds(r0, FILT_ROWS), :] = k
        return ss + jnp.sum(k * k, axis=0, keepdims=True)

    sumsq = lax.fori_loop(0, nchunk, taps, jnp.zeros((1, LANES), F32))
    scale = lax.rsqrt(sumsq + EPS) * F32(1.0 / (2 * SEQ))

    def stage1(j, carry):
        n2 = 2 * j
        off0 = pl.multiple_of(n2 * N1C, N1C)
        off1 = pl.multiple_of((n2 + 1) * N1C, N1C)
        x = jnp.concatenate([k_ref[pl.ds(off0, N1C), :], k_ref[pl.ds(off1, N1C), :]], axis=1)
        a = jnp.dot(m1_ref[...], x.astype(BF16), preferred_element_type=F32)
        _store_stage1(a_ref, a, N1C, n2, twc_ref, tws_ref)
        return carry

    lax.fori_loop(0, N2 // 2, stage1, 0)

    def stage2(j, carry):
        k1 = 2 * j
        x, _, _ = _load_slab_pair(a_ref, k1)
        y = jnp.dot(m2_ref[...], x.astype(BF16), preferred_element_type=F32)
        o0 = pl.multiple_of(k1 * 2 * N2, 2 * N2)
        o_ref[0, pl.ds(o0, 2 * N2), :] = (y[:, :LANES] * scale).astype(BF16)
        o_ref[0, pl.ds(o0 + 2 * N2, 2 * N2), :] = (y[:, LANES:] * scale).astype(BF16)
        return carry

    lax.fori_loop(0, N1C // 2, stage2, 0)


def _filters(feat, w0, b0, wi, bi, fr, w_out, deltas, m1k, m2, twc, tws):
    nblk = HYENA_WIDTH // LANES
    rows = 2 * SEQ
    return pl.pallas_call(
        _filter_kernel,
        grid=(HYENA_ORDER * nblk,),
        in_specs=[_const_spec(feat.shape), _const_spec(w0.shape), _const_spec(b0.shape),
                  _const_spec(wi.shape), _const_spec(bi.shape), _const_spec(fr.shape),
                  pl.BlockSpec((FILTER_HIDDEN, LANES), lambda g: (0, (g // nblk) * 2 * nblk + g % nblk)),
                  pl.BlockSpec((FILTER_HIDDEN, LANES), lambda g: (0, (g // nblk) * 2 * nblk + nblk + g % nblk)),
                  pl.BlockSpec((1, LANES), lambda g: (0, g % nblk)),
                  _const_spec(m1k.shape), _const_spec(m2.shape),
                  _const_spec(twc.shape), _const_spec(tws.shape)],
        out_specs=pl.BlockSpec((1, N1C * 2 * N2, LANES), lambda g: (g // nblk, 0, g % nblk)),
        out_shape=jax.ShapeDtypeStruct((HYENA_ORDER, N1C * 2 * N2, HYENA_WIDTH), BF16),
        scratch_shapes=[pltpu.VMEM((rows, FILTER_HIDDEN), F32),
                        pltpu.VMEM((rows, LANES), F32),
                        pltpu.VMEM((N1C * SLAB, LANES), F32)],
        compiler_params=_params(('arbitrary',)),
        name='hyena_filters',
    )(feat, w0, b0, wi, bi, fr, w_out, w_out, deltas, m1k, m2, twc, tws)


def _conv_kernel(u_ref, g_ref, kf_ref, d_ref, m1_ref, m2_ref, m2i_ref, m1i_ref,
                 twc_ref, tws_ref, o_ref, a_ref):
    def stage1(j, carry):
        n2 = 2 * j
        cols = []
        for h in range(2):
            cols.append(jnp.concatenate(
                [u_ref[0, pl.ds(n2 + h, CHUNKS, stride=CHUNK_PITCH), :],
                 u_ref[1, pl.ds(n2 + h, CHUNKS, stride=CHUNK_PITCH), :]], axis=0))
        x = jnp.concatenate(cols, axis=1).astype(BF16)
        a = jnp.dot(m1_ref[...], x, preferred_element_type=F32)
        _store_stage1(a_ref, a, N1C, n2, twc_ref, tws_ref)
        return carry

    lax.fori_loop(0, N2 // 2, stage1, 0)

    def stage2(j, carry):
        k1 = 2 * j
        x, off0, off1 = _load_slab_pair(a_ref, k1)
        s = jnp.dot(m2_ref[...], x.astype(BF16), preferred_element_type=F32)
        ko = pl.multiple_of(k1 * 2 * N2, 2 * N2)
        kf = kf_ref[0, pl.ds(ko, 4 * N2), :].astype(F32)
        kr = jnp.concatenate([kf[0:N2], kf[2 * N2:3 * N2]], axis=1)
        ki = jnp.concatenate([kf[N2:2 * N2], kf[3 * N2:4 * N2]], axis=1)
        xr = s[:N2]
        xi = s[N2:]
        y = jnp.concatenate([xr * kr - xi * ki, xr * ki + xi * kr], axis=0).astype(BF16)
        b = jnp.dot(m2i_ref[...], y, preferred_element_type=F32)
        a_ref[pl.ds(off0, 2 * N2), :] = b[:, :LANES]
        a_ref[pl.ds(off1, 2 * N2), :] = b[:, LANES:]
        return carry

    lax.fori_loop(0, N1C // 2, stage2, 0)

    d = d_ref[0]

    def stage3(j, carry):
        t2 = 2 * j
        cols = []
        for h in range(2):
            br = a_ref[pl.ds(t2 + h, N1C, stride=SLAB), :]
            bi = a_ref[pl.ds(N2 + t2 + h, N1C, stride=SLAB), :]
            row0 = pl.multiple_of((t2 + h) * N1C, N1C)
            c = twc_ref[pl.ds(row0, N1C), :]
            s = tws_ref[pl.ds(row0, N1C), :]
            cols.append(jnp.concatenate([br * c - bi * s, bi * c + br * s], axis=0))
        x = jnp.concatenate(cols, axis=1).astype(BF16)
        y = jnp.dot(m1i_ref[...], x, preferred_element_type=F32)
        for h in range(2):
            for b in range(2):
                yy = y[b * CHUNKS:(b + 1) * CHUNKS, h * LANES:(h + 1) * LANES]
                uu = u_ref[b, pl.ds(t2 + h, CHUNKS, stride=CHUNK_PITCH), :]
                gg = g_ref[b, pl.ds(t2 + h, CHUNKS, stride=CHUNK_PITCH), :]
                o_ref[b, pl.ds(t2 + h, CHUNKS, stride=CHUNK_PITCH), :] = gg * (yy + d * uu)
        return carry

    lax.fori_loop(0, N2 // 2, stage3, 0)
    for b in range(2):
        for i in range(CHUNK_PITCH - N2):
            o_ref[b, pl.ds(N2 + i, CHUNKS, stride=CHUNK_PITCH), :] = jnp.zeros((CHUNKS, LANES), F32)


def _longconv(u_arr, u_blk, g_arr, g_blk, kf, order, long_d, m1c, m2, m2i, m1i, twc, tws):
    nblk = HYENA_WIDTH // LANES
    blk = (2, SEQ_ROWS, LANES)
    return pl.pallas_call(
        _conv_kernel,
        grid=(nblk, BATCH // 2),
        in_specs=[pl.BlockSpec(blk, lambda c, b: (b, 0, u_blk + c)),
                  pl.BlockSpec(blk, lambda c, b: (b, 0, g_blk + c)),
                  pl.BlockSpec((1, N1C * 2 * N2, LANES), lambda c, b: (order, 0, c)),
                  pl.BlockSpec((1, 1, LANES), lambda c, b: (order, 0, c)),
                  _const_spec(m1c.shape), _const_spec(m2.shape), _const_spec(m2i.shape),
                  _const_spec(m1i.shape), _const_spec(twc.shape), _const_spec(tws.shape)],
        out_specs=pl.BlockSpec(blk, lambda c, b: (b, 0, c)),
        out_shape=jax.ShapeDtypeStruct((BATCH, SEQ_ROWS, HYENA_WIDTH), F32),
        scratch_shapes=[pltpu.VMEM((N1C * SLAB, LANES), F32)],
        compiler_params=_params(('parallel', 'parallel')),
        name=f'hyena_conv{order}',
    )(u_arr, g_arr, kf, long_d.reshape(HYENA_ORDER, 1, HYENA_WIDTH), m1c, m2, m2i, m1i, twc, tws)


FF_CHUNK = 1024


def _tail_kernel(x_ref, yf_ref, yh_ref, gf_ref, gh_ref, wo_ref, gm_ref, w1_ref, w2_ref, gl_ref, o_ref):
    def norm(v, g):
        return v * lax.rsqrt(jnp.mean(v * v, axis=-1, keepdims=True) + EPS) * g

    yf = jnp.concatenate([yf_ref[0, j * FPITCH:j * FPITCH + N1F, :]
                          for j in range(ROW_TILE // N1F)], axis=0)
    yh = jnp.concatenate([yh_ref[0, j * CHUNK_PITCH:j * CHUNK_PITCH + N2, :]
                          for j in range(TILE_CHUNKS)], axis=0)
    y = jnp.concatenate([norm(yf, gf_ref[...]), norm(yh, gh_ref[...])], axis=1).astype(BF16)
    xm = x_ref[0] + jnp.dot(y, wo_ref[...], preferred_element_type=F32)
    hm = norm(xm, gm_ref[...]).astype(BF16)
    acc = xm
    for c in range(D_FF // FF_CHUNK):
        a = jnp.dot(hm, w1_ref[:, c * FF_CHUNK:(c + 1) * FF_CHUNK], preferred_element_type=F32)
        a = jnp.square(jnp.maximum(a, 0.0)).astype(BF16)
        acc = acc + jnp.dot(a, w2_ref[c * FF_CHUNK:(c + 1) * FF_CHUNK, :], preferred_element_type=F32)
    o_ref[0] = norm(acc, gl_ref[...])


def _tail(x, yf, yh, g_f, g_h, wo, g_mlp, w1, w2, g_final):
    row = lambda v: v.reshape(1, -1)
    return pl.pallas_call(
        _tail_kernel,
        grid=(BATCH, SEQ // ROW_TILE),
        in_specs=[pl.BlockSpec((1, ROW_TILE, D_MODEL), lambda b, i: (b, i, 0)),
                  pl.BlockSpec((1, ROW_TILE // N1F * FPITCH, FOURIER_WIDTH), lambda b, i: (b, i, 0)),
                  pl.BlockSpec((1, TILE_CHUNKS * CHUNK_PITCH, HYENA_WIDTH), lambda b, i: (b, i, 0)),
                  _const_spec((1, FOURIER_WIDTH)), _const_spec((1, HYENA_WIDTH)),
                  _const_spec((D_MODEL, D_MODEL)), _const_spec((1, D_MODEL)),
                  _const_spec((D_MODEL, D_FF)), _const_spec((D_FF, D_MODEL)),
                  _const_spec((1, D_MODEL))],
        out_specs=pl.BlockSpec((1, ROW_TILE, D_MODEL), lambda b, i: (b, i, 0)),
        out_shape=jax.ShapeDtypeStruct((BATCH, SEQ, D_MODEL), F32),
        compiler_params=_params(('parallel', 'parallel')),
        name='out_proj_mlp',
    )(x, yf, yh, row(g_f), row(g_h), wo, row(g_mlp), w1, w2, row(g_final))


def _filter_features():
    r = jnp.arange(2 * SEQ, dtype=jnp.int32)
    p = (r % N1C) * N2 + r // N1C
    pos = jnp.where(p <= SEQ, p, 2 * SEQ - p).astype(F32)
    t = pos / F32(SEQ - 1)
    bands = jnp.linspace(1e-4, POS_BANDS - 1, POS_BANDS, dtype=F32)
    ang = (2.0 * jnp.pi * pos / SEQ)[:, None] * bands[None, :]
    z = jnp.concatenate([t[:, None], jnp.cos(ang), -jnp.sin(ang)], axis=-1)
    return jnp.pad(z, ((0, 0), (0, LANES - POS_EMB_DIM)))


def kernel(x, g_mix, w_in, conv_w, conv_b, filt_w0, filt_b0, filt_w_inner, filt_b_inner, filt_freq,
           filt_w_out, long_d, g_fourier, g_hyena, w_out, g_mlp, w_fc1, w_fc2, g_final):
    tb = _tables()
    bf = lambda v: v.astype(BF16)
    twc_f, tws_f = _twiddles(N1F, SEQ)
    twc_c, tws_c = _twiddles(N1C, 2 * SEQ)

    w0 = jnp.pad(filt_w0[0], ((0, LANES - POS_EMB_DIM), (0, 0)))
    deltas = jnp.linspace(MIN_DECAY, MAX_DECAY, HYENA_WIDTH, dtype=F32).reshape(1, HYENA_WIDTH)
    kf = _filters(_filter_features(), w0, filt_b0[0].reshape(1, -1), filt_w_inner[0],
                  filt_b_inner[0].reshape(FILTER_INNER_LAYERS, 1, FILTER_HIDDEN),
                  filt_freq[0].reshape(1, -1), filt_w_out[0], deltas,
                  bf(tb['m1k']), bf(tb['m2']), twc_c, tws_c)

    w_prep = _prep_weights(w_in[0], g_mix[0], tb['chan'])
    p = _inproj(x, w_prep)
    yf = _fourier(p, bf(tb['m1f']), bf(tb['m2r']), twc_f, tws_f)
    hy = _shortconv(p, conv_w[0], conv_b[0])
    nblk = HYENA_WIDTH // LANES
    fft = (bf(tb['m1c']), bf(tb['m2']), bf(tb['m2i']), bf(tb['m1i']), twc_c, tws_c)
    z1 = _longconv(hy, 2 * nblk, hy, 0, kf, 0, long_d[0], *fft)
    z2 = _longconv(z1, 0, hy, nblk, kf, 1, long_d[0], *fft)

    return _tail(x, yf, z2, g_fourier[0], g_hyena[0], bf(w_out[0]), g_mlp[0],
                 bf(w_fc1[0]), bf(w_fc2[0]), g_final)
```

```python
import functools
import math

import numpy as np
import jax
import jax.numpy as jnp
from jax import lax
from jax.experimental import pallas as pl
from jax.experimental.pallas import tpu as pltpu

F32 = jnp.float32
BF16 = jnp.bfloat16

D_MODEL = 1024
BATCH = 8
SEQ = 4096
FOURIER_WIDTH = 512
FOURIER_GROUP_DIM = 64
HYENA_WIDTH = 512
HYENA_ORDER = 2
POS_BANDS = 16
POS_EMB_DIM = 1 + 2 * POS_BANDS
FILTER_HIDDEN = 64
FILTER_INNER_LAYERS = 2
D_FF = 4 * D_MODEL
DECAY_TARGET = 1e-2
MAX_DECAY = math.log(DECAY_TARGET) / 0.3
MIN_DECAY = math.log(DECAY_TARGET) / 1.5
EPS = 1e-5

LANES = 128
N2 = 128
CHUNKS = SEQ // N2
CHUNK_PITCH = N2 + 8
SEQ_ROWS = CHUNKS * CHUNK_PITCH
N1F = SEQ // N2
N1C = 2 * SEQ // N2
SLAB = 2 * N2 + 8
FPITCH = N1F + 8
FROWS = N2 * FPITCH
IN_COLS = 2 * FOURIER_WIDTH + 3 * HYENA_WIDTH
VMEM_LIMIT = 60 * 1024 * 1024
STAGE1_UNROLL = 8
STAGE2_UNROLL = 4


def _cs(rows, cols, n):
    m = (np.outer(np.arange(rows), np.arange(cols)) % n).astype(np.float64)
    ang = 2.0 * np.pi * m / n
    return np.cos(ang), np.sin(ang)


def _tables():
    t = {}
    c, s = _cs(FOURIER_GROUP_DIM, FOURIER_GROUP_DIM, FOURIER_GROUP_DIM)
    eye = np.eye(FOURIER_WIDTH // FOURIER_GROUP_DIM)
    scale = 1.0 / math.sqrt(SEQ * FOURIER_GROUP_DIM)
    t['chan'] = np.concatenate([np.kron(eye, c), -np.kron(eye, s)], axis=1) * scale
    c, s = _cs(N1F, N1F, N1F)
    t['m1f'] = np.block([[c, s], [-s, c]])
    c, s = _cs(N1C, CHUNKS, N1C)
    t['m1c'] = np.block([[c, s], [-s, c]])
    c, s = _cs(N1C, N1C, N1C)
    t['m1k'] = np.concatenate([c, -s], axis=0)
    c, s = _cs(CHUNKS, N1C, N1C)
    t['m1i'] = np.block([[c, -s], [s, c]])
    c, s = _cs(N2, N2, N2)
    t['m2'] = np.block([[c, s], [-s, c]])
    t['m2i'] = np.block([[c, -s], [s, c]])
    t['m2r'] = np.concatenate([c, s], axis=1)
    return {k: jnp.asarray(v.astype(np.float32)) for k, v in t.items()}


def _twiddles(n1, n):
    r = lax.broadcasted_iota(jnp.int32, (N2 * n1, LANES), 0)
    m = (r // n1) * (r % n1)
    ang = m.astype(F32) * F32(2.0 * math.pi / n)
    return jnp.cos(ang), jnp.sin(ang)


def _const_spec(shape):
    nd = len(shape)
    return pl.BlockSpec(shape, lambda *_: (0,) * nd, pipeline_mode=pl.Buffered(1))


def _params(sem):
    return pltpu.CompilerParams(dimension_semantics=sem, vmem_limit_bytes=VMEM_LIMIT)


def _prep_kernel(w_ref, g_ref, chan_ref, o_ref):
    g = g_ref[...]
    wf = w_ref[:, :FOURIER_WIDTH]
    z = jnp.dot(wf, chan_ref[...], preferred_element_type=F32,
                precision=lax.Precision.HIGHEST)
    o_ref[:, :2 * FOURIER_WIDTH] = (g * z).astype(BF16)
    o_ref[:, 2 * FOURIER_WIDTH:] = (g * w_ref[:, FOURIER_WIDTH:]).astype(BF16)


def _prep_weights(w_in, g_mix, chan):
    return pl.pallas_call(
        _prep_kernel,
        out_shape=jax.ShapeDtypeStruct((D_MODEL, IN_COLS), BF16),
        compiler_params=_params(None),
        name='prep_weights',
    )(w_in, g_mix.reshape(D_MODEL, 1), chan)


ROW_TILE = 512
TILE_CHUNKS = ROW_TILE // N2


def _inproj_kernel(x_ref, w_ref, o_ref):
    x = x_ref[0]
    r = lax.rsqrt(jnp.mean(x * x, axis=-1, keepdims=True) + EPS)
    h = (x * r).astype(BF16)
    p = jnp.dot(h, w_ref[...], preferred_element_type=F32)
    for j in range(TILE_CHUNKS):
        o_ref[0, j * CHUNK_PITCH:j * CHUNK_PITCH + N2, :] = p[j * N2:(j + 1) * N2]
        o_ref[0, j * CHUNK_PITCH + N2:(j + 1) * CHUNK_PITCH, :] = jnp.zeros(
            (CHUNK_PITCH - N2, IN_COLS), F32)


def _inproj(x, w):
    return pl.pallas_call(
        _inproj_kernel,
        grid=(BATCH, SEQ // ROW_TILE),
        in_specs=[pl.BlockSpec((1, ROW_TILE, D_MODEL), lambda b, i: (b, i, 0)),
                  _const_spec((D_MODEL, IN_COLS))],
        out_specs=pl.BlockSpec((1, TILE_CHUNKS * CHUNK_PITCH, IN_COLS), lambda b, i: (b, i, 0)),
        out_shape=jax.ShapeDtypeStruct((BATCH, SEQ_ROWS, IN_COLS), F32),
        compiler_params=_params(('parallel', 'parallel')),
        name='in_proj',
    )(x, w)


def _shortconv_kernel(p_ref, w_ref, b_ref, o_ref):
    w0 = w_ref[0:1, :]
    w1 = w_ref[1:2, :]
    w2 = w_ref[2:3, :]
    bias = b_ref[...]
    row = lax.broadcasted_iota(jnp.int32, (N2, LANES), 0)
    zero_row = jnp.zeros((1, LANES), F32)
    for n1 in range(CHUNKS):
        base = n1 * CHUNK_PITCH
        cur = p_ref[0, base:base + N2, :]
        before = p_ref[0, base - CHUNK_PITCH + N2 - 1:base - CHUNK_PITCH + N2, :] if n1 > 0 else zero_row
        after = p_ref[0, base + CHUNK_PITCH:base + CHUNK_PITCH + 1, :] if n1 < CHUNKS - 1 else zero_row
        up = jnp.where(row == 0, before, pltpu.roll(cur, 1, axis=0))
        dn = jnp.where(row == N2 - 1, after, pltpu.roll(cur, N2 - 1, axis=0))
        o_ref[0, base:base + N2, :] = up * w0 + cur * w1 + dn * w2 + bias
        o_ref[0, base + N2:base + CHUNK_PITCH, :] = jnp.zeros((CHUNK_PITCH - N2, LANES), F32)


def _shortconv(p, conv_w, conv_b):
    nblk = 3 * HYENA_WIDTH // LANES
    first = 2 * FOURIER_WIDTH // LANES
    return pl.pallas_call(
        _shortconv_kernel,
        grid=(BATCH, nblk),
        in_specs=[pl.BlockSpec((1, SEQ_ROWS, LANES), lambda b, c: (b, 0, first + c)),
                  pl.BlockSpec((3, LANES), lambda b, c: (0, c)),
                  pl.BlockSpec((1, LANES), lambda b, c: (0, c))],
        out_specs=pl.BlockSpec((1, SEQ_ROWS, LANES), lambda b, c: (b, 0, c)),
        out_shape=jax.ShapeDtypeStruct((BATCH, SEQ_ROWS, 3 * HYENA_WIDTH), F32),
        compiler_params=_params(('parallel', 'parallel')),
        name='short_conv',
    )(p, conv_w, conv_b.reshape(1, -1))


def _store_stage1(a_ref, a, n1, n2, twc_ref, tws_ref):
    for h in range(2):
        ar = a[:n1, h * LANES:(h + 1) * LANES]
        ai = a[n1:, h * LANES:(h + 1) * LANES]
        row0 = pl.multiple_of((n2 + h) * n1, n1)
        c = twc_ref[pl.ds(row0, n1), :]
        s = tws_ref[pl.ds(row0, n1), :]
        a_ref[pl.ds(n2 + h, n1, stride=SLAB), :] = ar * c + ai * s
        a_ref[pl.ds(N2 + n2 + h, n1, stride=SLAB), :] = ai * c - ar * s


def _load_slab_pair(a_ref, k1):
    off0 = pl.multiple_of(k1 * SLAB, 8)
    off1 = pl.multiple_of((k1 + 1) * SLAB, 8)
    x = jnp.concatenate([a_ref[pl.ds(off0, 2 * N2), :], a_ref[pl.ds(off1, 2 * N2), :]], axis=1)
    return x, off0, off1


def _fourier_kernel(zr_ref, zi_ref, m1_ref, m2_ref, twc_ref, tws_ref, o_ref, a_ref):
    def stage1(j, carry):
        n2 = 2 * j
        cols = []
        for h in range(2):
            cols.append(jnp.concatenate(
                [zr_ref[0, pl.ds(n2 + h, N1F, stride=CHUNK_PITCH), :],
                 zi_ref[0, pl.ds(n2 + h, N1F, stride=CHUNK_PITCH), :]], axis=0))
        x = jnp.concatenate(cols, axis=1).astype(BF16)
        a = jnp.dot(m1_ref[...], x, preferred_element_type=F32)
        _store_stage1(a_ref, a, N1F, n2, twc_ref, tws_ref)
        return carry

    lax.fori_loop(0, N2 // 2, stage1, 0, unroll=STAGE1_UNROLL)

    def stage2(j, carry):
        k1 = 2 * j
        x, _, _ = _load_slab_pair(a_ref, k1)
        y = jnp.dot(m2_ref[...], x.astype(BF16), preferred_element_type=F32)
        o_ref[0, pl.ds(k1, N2, stride=FPITCH), :] = y[:, :LANES]
        o_ref[0, pl.ds(k1 + 1, N2, stride=FPITCH), :] = y[:, LANES:]
        return carry

    lax.fori_loop(0, N1F // 2, stage2, 0, unroll=STAGE2_UNROLL)
    for i in range(FPITCH - N1F):
        o_ref[0, pl.ds(N1F + i, N2, stride=FPITCH), :] = jnp.zeros((N2, LANES), F32)


def _fourier(p, m1, m2r, twc, tws):
    nblk = FOURIER_WIDTH // LANES
    return pl.pallas_call(
        _fourier_kernel,
        grid=(BATCH, nblk),
        in_specs=[pl.BlockSpec((1, SEQ_ROWS, LANES), lambda b, c: (b, 0, c)),
                  pl.BlockSpec((1, SEQ_ROWS, LANES), lambda b, c: (b, 0, nblk + c)),
                  _const_spec(m1.shape), _const_spec(m2r.shape),
                  _const_spec(twc.shape), _const_spec(tws.shape)],
        out_specs=pl.BlockSpec((1, FROWS, LANES), lambda b, c: (b, 0, c)),
        out_shape=jax.ShapeDtypeStruct((BATCH, FROWS, FOURIER_WIDTH), F32),
        scratch_shapes=[pltpu.VMEM((N1F * SLAB, LANES), F32)],
        compiler_params=_params(('parallel', 'parallel')),
        name='fourier_fft',
    )(p, p, m1, m2r, twc, tws)


FILT_ROWS = 512


def _filter_kernel(feat_ref, w0_ref, b0_ref, wi_ref, bi_ref, fr_ref, wf_ref, wb_ref, dl_ref,
                   m1_ref, m2_ref, twc_ref, tws_ref, o_ref, h_ref, k_ref, a_ref):
    hi = lax.Precision.HIGHEST
    nchunk = 2 * SEQ // FILT_ROWS

    @pl.when(pl.program_id(0) == 0)
    def _():
        fr = fr_ref[...]

        def mlp(i, carry):
            r0 = pl.multiple_of(i * FILT_ROWS, FILT_ROWS)
            h = jnp.sin(fr * (jnp.dot(feat_ref[pl.ds(r0, FILT_ROWS), :], w0_ref[...],
                                      preferred_element_type=F32, precision=hi) + b0_ref[...]))
            for j in range(FILTER_INNER_LAYERS):
                h = jnp.sin(fr * (jnp.dot(h, wi_ref[j], preferred_element_type=F32, precision=hi)
                                  + bi_ref[j]))
            h_ref[pl.ds(r0, FILT_ROWS), :] = h
            return carry

        lax.fori_loop(0, nchunk, mlp, 0)

    def taps(i, ss):
        r0 = pl.multiple_of(i * FILT_ROWS, FILT_ROWS)
        r = r0 + lax.broadcasted_iota(jnp.int32, (FILT_ROWS, LANES), 0)
        p = (r % N1C) * N2 + r // N1C
        pos = jnp.where(p <= SEQ, p, 2 * SEQ - p)
        t = pos.astype(F32) / F32(SEQ - 1)
        decay = jnp.exp(-t * jnp.abs(dl_ref[...]))
        h = h_ref[pl.ds(r0, FILT_ROWS), :]
        kf = jnp.dot(h, wf_ref[...], preferred_element_type=F32, precision=hi)
        kb = jnp.dot(h, wb_ref[...], preferred_element_type=F32, precision=hi)
        k = jnp.where(p < SEQ, kf, jnp.where(p == SEQ, 0.0, kb)) + jnp.where(p == 0, kb, 0.0)
        k = k * decay
        k_ref[pl.ds(r0, FILT_ROWS), :] = k
        return ss + jnp.sum(k * k, axis=0, keepdims=True)

    sumsq = lax.fori_loop(0, nchunk, taps, jnp.zeros((1, LANES), F32))
    scale = lax.rsqrt(sumsq + EPS) * F32(1.0 / (2 * SEQ))

    def stage1(j, carry):
        n2 = 2 * j
        off0 = pl.multiple_of(n2 * N1C, N1C)
        off1 = pl.multiple_of((n2 + 1) * N1C, N1C)
        x = jnp.concatenate([k_ref[pl.ds(off0, N1C), :], k_ref[pl.ds(off1, N1C), :]], axis=1)
        a = jnp.dot(m1_ref[...], x.astype(BF16), preferred_element_type=F32)
        _store_stage1(a_ref, a, N1C, n2, twc_ref, tws_ref)
        return carry

    lax.fori_loop(0, N2 // 2, stage1, 0, unroll=STAGE1_UNROLL)

    def stage2(j, carry):
        k1 = 2 * j
        x, _, _ = _load_slab_pair(a_ref, k1)
        y = jnp.dot(m2_ref[...], x.astype(BF16), preferred_element_type=F32)
        o0 = pl.multiple_of(k1 * 2 * N2, 2 * N2)
        o_ref[0, pl.ds(o0, 2 * N2), :] = (y[:, :LANES] * scale).astype(BF16)
        o_ref[0, pl.ds(o0 + 2 * N2, 2 * N2), :] = (y[:, LANES:] * scale).astype(BF16)
        return carry

    lax.fori_loop(0, N1C // 2, stage2, 0, unroll=STAGE2_UNROLL)


def _filters(feat, w0, b0, wi, bi, fr, w_out, deltas, m1k, m2, twc, tws):
    nblk = HYENA_WIDTH // LANES
    rows = 2 * SEQ
    return pl.pallas_call(
        _filter_kernel,
        grid=(HYENA_ORDER * nblk,),
        in_specs=[_const_spec(feat.shape), _const_spec(w0.shape), _const_spec(b0.shape),
                  _const_spec(wi.shape), _const_spec(bi.shape), _const_spec(fr.shape),
                  pl.BlockSpec((FILTER_HIDDEN, LANES), lambda g: (0, (g // nblk) * 2 * nblk + g % nblk)),
                  pl.BlockSpec((FILTER_HIDDEN, LANES), lambda g: (0, (g // nblk) * 2 * nblk + nblk + g % nblk)),
                  pl.BlockSpec((1, LANES), lambda g: (0, g % nblk)),
                  _const_spec(m1k.shape), _const_spec(m2.shape),
                  _const_spec(twc.shape), _const_spec(tws.shape)],
        out_specs=pl.BlockSpec((1, N1C * 2 * N2, LANES), lambda g: (g // nblk, 0, g % nblk)),
        out_shape=jax.ShapeDtypeStruct((HYENA_ORDER, N1C * 2 * N2, HYENA_WIDTH), BF16),
        scratch_shapes=[pltpu.VMEM((rows, FILTER_HIDDEN), F32),
                        pltpu.VMEM((rows, LANES), F32),
                        pltpu.VMEM((N1C * SLAB, LANES), F32)],
        compiler_params=_params(('arbitrary',)),
        name='hyena_filters',
    )(feat, w0, b0, wi, bi, fr, w_out, w_out, deltas, m1k, m2, twc, tws)


def _conv_kernel(u_ref, g_ref, kf_ref, d_ref, m1_ref, m2_ref, m2i_ref, m1i_ref,
                 twc_ref, tws_ref, o_ref, a_ref):
    def stage1(j, carry):
        n2 = 2 * j
        cols = []
        for h in range(2):
            cols.append(jnp.concatenate(
                [u_ref[0, pl.ds(n2 + h, CHUNKS, stride=CHUNK_PITCH), :],
                 u_ref[1, pl.ds(n2 + h, CHUNKS, stride=CHUNK_PITCH), :]], axis=0))
        x = jnp.concatenate(cols, axis=1).astype(BF16)
        a = jnp.dot(m1_ref[...], x, preferred_element_type=F32)
        _store_stage1(a_ref, a, N1C, n2, twc_ref, tws_ref)
        return carry

    lax.fori_loop(0, N2 // 2, stage1, 0, unroll=STAGE1_UNROLL)

    def stage2(j, carry):
        k1 = 2 * j
        x, off0, off1 = _load_slab_pair(a_ref, k1)
        s = jnp.dot(m2_ref[...], x.astype(BF16), preferred_element_type=F32)
        ko = pl.multiple_of(k1 * 2 * N2, 2 * N2)
        kf = kf_ref[0, pl.ds(ko, 4 * N2), :].astype(F32)
        kr = jnp.concatenate([kf[0:N2], kf[2 * N2:3 * N2]], axis=1)
        ki = jnp.concatenate([kf[N2:2 * N2], kf[3 * N2:4 * N2]], axis=1)
        xr = s[:N2]
        xi = s[N2:]
        y = jnp.concatenate([xr * kr - xi * ki, xr * ki + xi * kr], axis=0).astype(BF16)
        b = jnp.dot(m2i_ref[...], y, preferred_element_type=F32)
        a_ref[pl.ds(off0, 2 * N2), :] = b[:, :LANES]
        a_ref[pl.ds(off1, 2 * N2), :] = b[:, LANES:]
        return carry

    lax.fori_loop(0, N1C // 2, stage2, 0, unroll=STAGE2_UNROLL)

    d = d_ref[0]

    def stage3(j, carry):
        t2 = 2 * j
        cols = []
        for h in range(2):
            br = a_ref[pl.ds(t2 + h, N1C, stride=SLAB), :]
            bi = a_ref[pl.ds(N2 + t2 + h, N1C, stride=SLAB), :]
            row0 = pl.multiple_of((t2 + h) * N1C, N1C)
            c = twc_ref[pl.ds(row0, N1C), :]
            s = tws_ref[pl.ds(row0, N1C), :]
            cols.append(jnp.concatenate([br * c - bi * s, bi * c + br * s], axis=0))
        x = jnp.concatenate(cols, axis=1).astype(BF16)
        y = jnp.dot(m1i_ref[...], x, preferred_element_type=F32)
        for h in range(2):
            for b in range(2):
                yy = y[b * CHUNKS:(b + 1) * CHUNKS, h * LANES:(h + 1) * LANES]
                uu = u_ref[b, pl.ds(t2 + h, CHUNKS, stride=CHUNK_PITCH), :]
                gg = g_ref[b, pl.ds(t2 + h, CHUNKS, stride=CHUNK_PITCH), :]
                o_ref[b, pl.ds(t2 + h, CHUNKS, stride=CHUNK_PITCH), :] = gg * (yy + d * uu)
        return carry

    lax.fori_loop(0, N2 // 2, stage3, 0, unroll=STAGE1_UNROLL)
    for b in range(2):
        for i in range(CHUNK_PITCH - N2):
            o_ref[b, pl.ds(N2 + i, CHUNKS, stride=CHUNK_PITCH), :] = jnp.zeros((CHUNKS, LANES), F32)


def _longconv(u_arr, u_blk, g_arr, g_blk, kf, order, long_d, m1c, m2, m2i, m1i, twc, tws):
    nblk = HYENA_WIDTH // LANES
    blk = (2, SEQ_ROWS, LANES)
    return pl.pallas_call(
        _conv_kernel,
        grid=(nblk, BATCH // 2),
        in_specs=[pl.BlockSpec(blk, lambda c, b: (b, 0, u_blk + c)),
                  pl.BlockSpec(blk, lambda c, b: (b, 0, g_blk + c)),
                  pl.BlockSpec((1, N1C * 2 * N2, LANES), lambda c, b: (order, 0, c)),
                  pl.BlockSpec((1, 1, LANES), lambda c, b: (order, 0, c)),
                  _const_spec(m1c.shape), _const_spec(m2.shape), _const_spec(m2i.shape),
                  _const_spec(m1i.shape), _const_spec(twc.shape), _const_spec(tws.shape)],
        out_specs=pl.BlockSpec(blk, lambda c, b: (b, 0, c)),
        out_shape=jax.ShapeDtypeStruct((BATCH, SEQ_ROWS, HYENA_WIDTH), F32),
        scratch_shapes=[pltpu.VMEM((N1C * SLAB, LANES), F32)],
        compiler_params=_params(('parallel', 'parallel')),
        name=f'hyena_conv{order}',
    )(u_arr, g_arr, kf, long_d.reshape(HYENA_ORDER, 1, HYENA_WIDTH), m1c, m2, m2i, m1i, twc, tws)


FF_CHUNK = 1024


def _tail_kernel(x_ref, yf_ref, yh_ref, gf_ref, gh_ref, wo_ref, gm_ref, w1_ref, w2_ref, gl_ref, o_ref):
    def norm(v, g):
        return v * lax.rsqrt(jnp.mean(v * v, axis=-1, keepdims=True) + EPS) * g

    yf = jnp.concatenate([yf_ref[0, j * FPITCH:j * FPITCH + N1F, :]
                          for j in range(ROW_TILE // N1F)], axis=0)
    yh = jnp.concatenate([yh_ref[0, j * CHUNK_PITCH:j * CHUNK_PITCH + N2, :]
                          for j in range(TILE_CHUNKS)], axis=0)
    y = jnp.concatenate([norm(yf, gf_ref[...]), norm(yh, gh_ref[...])], axis=1).astype(BF16)
    xm = x_ref[0] + jnp.dot(y, wo_ref[...], preferred_element_type=F32)
    hm = norm(xm, gm_ref[...]).astype(BF16)
    acc = xm
    for c in range(D_FF // FF_CHUNK):
        a = jnp.dot(hm, w1_ref[:, c * FF_CHUNK:(c + 1) * FF_CHUNK], preferred_element_type=F32)
        a = jnp.square(jnp.maximum(a, 0.0)).astype(BF16)
        acc = acc + jnp.dot(a, w2_ref[c * FF_CHUNK:(c + 1) * FF_CHUNK, :], preferred_element_type=F32)
    o_ref[0] = norm(acc, gl_ref[...])


def _tail(x, yf, yh, g_f, g_h, wo, g_mlp, w1, w2, g_final):
    row = lambda v: v.reshape(1, -1)
    return pl.pallas_call(
        _tail_kernel,
        grid=(BATCH, SEQ // ROW_TILE),
        in_specs=[pl.BlockSpec((1, ROW_TILE, D_MODEL), lambda b, i: (b, i, 0)),
                  pl.BlockSpec((1, ROW_TILE // N1F * FPITCH, FOURIER_WIDTH), lambda b, i: (b, i, 0)),
                  pl.BlockSpec((1, TILE_CHUNKS * CHUNK_PITCH, HYENA_WIDTH), lambda b, i: (b, i, 0)),
                  _const_spec((1, FOURIER_WIDTH)), _const_spec((1, HYENA_WIDTH)),
                  _const_spec((D_MODEL, D_MODEL)), _const_spec((1, D_MODEL)),
                  _const_spec((D_MODEL, D_FF)), _const_spec((D_FF, D_MODEL)),
                  _const_spec((1, D_MODEL))],
        out_specs=pl.BlockSpec((1, ROW_TILE, D_MODEL), lambda b, i: (b, i, 0)),
        out_shape=jax.ShapeDtypeStruct((BATCH, SEQ, D_MODEL), F32),
        compiler_params=_params(('parallel', 'parallel')),
        name='out_proj_mlp',
    )(x, yf, yh, row(g_f), row(g_h), wo, row(g_mlp), w1, w2, row(g_final))


def _filter_features():
    r = jnp.arange(2 * SEQ, dtype=jnp.int32)
    p = (r % N1C) * N2 + r // N1C
    pos = jnp.where(p <= SEQ, p, 2 * SEQ - p).astype(F32)
    t = pos / F32(SEQ - 1)
    bands = jnp.linspace(1e-4, POS_BANDS - 1, POS_BANDS, dtype=F32)
    ang = (2.0 * jnp.pi * pos / SEQ)[:, None] * bands[None, :]
    z = jnp.concatenate([t[:, None], jnp.cos(ang), -jnp.sin(ang)], axis=-1)
    return jnp.pad(z, ((0, 0), (0, LANES - POS_EMB_DIM)))


def kernel(x, g_mix, w_in, conv_w, conv_b, filt_w0, filt_b0, filt_w_inner, filt_b_inner, filt_freq,
           filt_w_out, long_d, g_fourier, g_hyena, w_out, g_mlp, w_fc1, w_fc2, g_final):
    tb = _tables()
    bf = lambda v: v.astype(BF16)
    twc_f, tws_f = _twiddles(N1F, SEQ)
    twc_c, tws_c = _twiddles(N1C, 2 * SEQ)

    w0 = jnp.pad(filt_w0[0], ((0, LANES - POS_EMB_DIM), (0, 0)))
    deltas = jnp.linspace(MIN_DECAY, MAX_DECAY, HYENA_WIDTH, dtype=F32).reshape(1, HYENA_WIDTH)
    kf = _filters(_filter_features(), w0, filt_b0[0].reshape(1, -1), filt_w_inner[0],
                  filt_b_inner[0].reshape(FILTER_INNER_LAYERS, 1, FILTER_HIDDEN),
                  filt_freq[0].reshape(1, -1), filt_w_out[0], deltas,
                  bf(tb['m1k']), bf(tb['m2']), twc_c, tws_c)

    w_prep = _prep_weights(w_in[0], g_mix[0], tb['chan'])
    p = _inproj(x, w_prep)
    yf = _fourier(p, bf(tb['m1f']), bf(tb['m2r']), twc_f, tws_f)
    hy = _shortconv(p, conv_w[0], conv_b[0])
    nblk = HYENA_WIDTH // LANES
    fft = (bf(tb['m1c']), bf(tb['m2']), bf(tb['m2i']), bf(tb['m1i']), twc_c, tws_c)
    z1 = _longconv(hy, 2 * nblk, hy, 0, kf, 0, long_d[0], *fft)
    z2 = _longconv(z1, 0, hy, nblk, kf, 1, long_d[0], *fft)

    return _tail(x, yf, z2, g_fourier[0], g_hyena[0], bf(w_out[0]), g_mlp[0],
                 bf(w_fc1[0]), bf(w_fc2[0]), g_final)
```

```python
import functools
import math

import numpy as np
import jax
import jax.numpy as jnp
from jax import lax
from jax.experimental import pallas as pl
from jax.experimental.pallas import tpu as pltpu

F32 = jnp.float32
BF16 = jnp.bfloat16

D_MODEL = 1024
BATCH = 8
SEQ = 4096
FOURIER_WIDTH = 512
FOURIER_GROUP_DIM = 64
HYENA_WIDTH = 512
HYENA_ORDER = 2
POS_BANDS = 16
POS_EMB_DIM = 1 + 2 * POS_BANDS
FILTER_HIDDEN = 64
FILTER_INNER_LAYERS = 2
D_FF = 4 * D_MODEL
DECAY_TARGET = 1e-2
MAX_DECAY = math.log(DECAY_TARGET) / 0.3
MIN_DECAY = math.log(DECAY_TARGET) / 1.5
EPS = 1e-5

LANES = 128
N2 = 128
CHUNKS = SEQ // N2
CHUNK_PITCH = N2 + 8
SEQ_ROWS = CHUNKS * CHUNK_PITCH
N1F = SEQ // N2
N1C = 2 * SEQ // N2
SLAB = 2 * N2 + 8
FPITCH = N1F + 8
FROWS = N2 * FPITCH
IN_COLS = 2 * FOURIER_WIDTH + 3 * HYENA_WIDTH
VMEM_LIMIT = 60 * 1024 * 1024
STAGE1_UNROLL = 8
STAGE2_UNROLL = 4


def _cs(rows, cols, n):
    m = (np.outer(np.arange(rows), np.arange(cols)) % n).astype(np.float64)
    ang = 2.0 * np.pi * m / n
    return np.cos(ang), np.sin(ang)


def _tables():
    t = {}
    c, s = _cs(FOURIER_GROUP_DIM, FOURIER_GROUP_DIM, FOURIER_GROUP_DIM)
    eye = np.eye(FOURIER_WIDTH // FOURIER_GROUP_DIM)
    scale = 1.0 / math.sqrt(SEQ * FOURIER_GROUP_DIM)
    t['chan'] = np.concatenate([np.kron(eye, c), -np.kron(eye, s)], axis=1) * scale
    c, s = _cs(N1F, N1F, N1F)
    t['m1f'] = np.block([[c, s], [-s, c]])
    c, s = _cs(N1C, CHUNKS, N1C)
    t['m1c'] = np.block([[c, s], [-s, c]])
    c, s = _cs(N1C, N1C, N1C)
    t['m1k'] = np.concatenate([c, -s], axis=0)
    c, s = _cs(CHUNKS, N1C, N1C)
    t['m1i'] = np.block([[c, -s], [s, c]])
    c, s = _cs(N2, N2, N2)
    t['m2'] = np.block([[c, s], [-s, c]])
    t['m2i'] = np.block([[c, -s], [s, c]])
    t['m2r'] = np.concatenate([c, s], axis=1)
    return {k: jnp.asarray(v.astype(np.float32)) for k, v in t.items()}


def _twiddles(n1, n):
    r = np.arange(N2 * n1)
    ang = 2.0 * np.pi * ((r // n1) * (r % n1)).astype(np.float64) / n
    wide = lambda v: jnp.asarray(np.broadcast_to(v.astype(np.float32)[:, None], (N2 * n1, LANES)))
    return wide(np.cos(ang)), wide(np.sin(ang))


def _const_spec(shape):
    nd = len(shape)
    return pl.BlockSpec(shape, lambda *_: (0,) * nd, pipeline_mode=pl.Buffered(1))


def _params(sem):
    return pltpu.CompilerParams(dimension_semantics=sem, vmem_limit_bytes=VMEM_LIMIT)


def _prep_kernel(w_ref, g_ref, chan_ref, o_ref):
    g = g_ref[...]
    wf = w_ref[:, :FOURIER_WIDTH]
    z = jnp.dot(wf, chan_ref[...], preferred_element_type=F32,
                precision=lax.Precision.HIGHEST)
    o_ref[:, :2 * FOURIER_WIDTH] = (g * z).astype(BF16)
    o_ref[:, 2 * FOURIER_WIDTH:] = (g * w_ref[:, FOURIER_WIDTH:]).astype(BF16)


def _prep_weights(w_in, g_mix, chan):
    return pl.pallas_call(
        _prep_kernel,
        out_shape=jax.ShapeDtypeStruct((D_MODEL, IN_COLS), BF16),
        compiler_params=_params(None),
        name='prep_weights',
    )(w_in, g_mix.reshape(D_MODEL, 1), chan)


ROW_TILE = 512
TILE_CHUNKS = ROW_TILE // N2


def _inproj_kernel(x_ref, w_ref, o_ref):
    x = x_ref[0]
    r = lax.rsqrt(jnp.mean(x * x, axis=-1, keepdims=True) + EPS)
    h = (x * r).astype(BF16)
    p = jnp.dot(h, w_ref[...], preferred_element_type=F32)
    for j in range(TILE_CHUNKS):
        o_ref[0, j * CHUNK_PITCH:j * CHUNK_PITCH + N2, :] = p[j * N2:(j + 1) * N2]
        o_ref[0, j * CHUNK_PITCH + N2:(j + 1) * CHUNK_PITCH, :] = jnp.zeros(
            (CHUNK_PITCH - N2, IN_COLS), F32)


def _inproj(x, w):
    return pl.pallas_call(
        _inproj_kernel,
        grid=(BATCH, SEQ // ROW_TILE),
        in_specs=[pl.BlockSpec((1, ROW_TILE, D_MODEL), lambda b, i: (b, i, 0)),
                  _const_spec((D_MODEL, IN_COLS))],
        out_specs=pl.BlockSpec((1, TILE_CHUNKS * CHUNK_PITCH, IN_COLS), lambda b, i: (b, i, 0)),
        out_shape=jax.ShapeDtypeStruct((BATCH, SEQ_ROWS, IN_COLS), F32),
        compiler_params=_params(('parallel', 'parallel')),
        name='in_proj',
    )(x, w)


def _rows(ref, b, n2):
    return ref[b, pl.ds(n2, CHUNKS, stride=CHUNK_PITCH), :]


def _pair_loop(body):
    peeled = N2 // 2 - 1 - (N2 // 2 - 2) // STAGE1_UNROLL * STAGE1_UNROLL
    body(0)
    lax.fori_loop(1, N2 // 2 - peeled, lambda j, c: (body(j), c)[1], 0, unroll=STAGE1_UNROLL)
    for j in range(N2 // 2 - peeled, N2 // 2):
        body(j)


def _shortconv_pair(ref, b, j, w_ref, b_ref):
    n2 = 2 * j
    row = lax.broadcasted_iota(jnp.int32, (CHUNKS, LANES), 0)
    c0 = _rows(ref, b, n2)
    c1 = _rows(ref, b, n2 + 1)
    if isinstance(j, int) and j == 0:
        before = jnp.where(row == 0, 0.0, pltpu.roll(_rows(ref, b, N2 - 1), 1, axis=0))
    else:
        before = _rows(ref, b, n2 - 1)
    if isinstance(j, int) and j == N2 // 2 - 1:
        after = jnp.where(row == CHUNKS - 1, 0.0, pltpu.roll(_rows(ref, b, 0), CHUNKS - 1, axis=0))
    else:
        after = _rows(ref, b, n2 + 2)
    w0 = w_ref[0:1, :]
    w1 = w_ref[1:2, :]
    w2 = w_ref[2:3, :]
    bias = b_ref[...]
    return (before * w0 + c0 * w1 + c1 * w2 + bias,
            c0 * w0 + c1 * w1 + after * w2 + bias)


def _aligned(v, m):
    return v if isinstance(v, int) else pl.multiple_of(v, m)


def _store_stage1(a_ref, a, n1, n2, twc_ref, tws_ref):
    for h in range(2):
        ar = a[:n1, h * LANES:(h + 1) * LANES]
        ai = a[n1:, h * LANES:(h + 1) * LANES]
        row0 = _aligned((n2 + h) * n1, n1)
        c = twc_ref[pl.ds(row0, n1), :]
        s = tws_ref[pl.ds(row0, n1), :]
        a_ref[pl.ds(n2 + h, n1, stride=SLAB), :] = ar * c + ai * s
        a_ref[pl.ds(N2 + n2 + h, n1, stride=SLAB), :] = ai * c - ar * s


def _load_slab_pair(a_ref, k1):
    off0 = pl.multiple_of(k1 * SLAB, 8)
    off1 = pl.multiple_of((k1 + 1) * SLAB, 8)
    x = jnp.concatenate([a_ref[pl.ds(off0, 2 * N2), :], a_ref[pl.ds(off1, 2 * N2), :]], axis=1)
    return x, off0, off1


def _fourier_kernel(zr_ref, zi_ref, m1_ref, m2_ref, twc_ref, tws_ref, o_ref, a_ref):
    def stage1(j, carry):
        n2 = 2 * j
        cols = []
        for h in range(2):
            cols.append(jnp.concatenate(
                [zr_ref[0, pl.ds(n2 + h, N1F, stride=CHUNK_PITCH), :],
                 zi_ref[0, pl.ds(n2 + h, N1F, stride=CHUNK_PITCH), :]], axis=0))
        x = jnp.concatenate(cols, axis=1).astype(BF16)
        a = jnp.dot(m1_ref[...], x, preferred_element_type=F32)
        _store_stage1(a_ref, a, N1F, n2, twc_ref, tws_ref)
        return carry

    lax.fori_loop(0, N2 // 2, stage1, 0, unroll=STAGE1_UNROLL)

    def stage2(j, carry):
        k1 = 2 * j
        x, _, _ = _load_slab_pair(a_ref, k1)
        y = jnp.dot(m2_ref[...], x.astype(BF16), preferred_element_type=F32)
        o_ref[0, pl.ds(k1, N2, stride=FPITCH), :] = y[:, :LANES]
        o_ref[0, pl.ds(k1 + 1, N2, stride=FPITCH), :] = y[:, LANES:]
        return carry

    lax.fori_loop(0, N1F // 2, stage2, 0, unroll=STAGE2_UNROLL)
    for i in range(FPITCH - N1F):
        o_ref[0, pl.ds(N1F + i, N2, stride=FPITCH), :] = jnp.zeros((N2, LANES), F32)


def _fourier(p, m1, m2r, twc, tws):
    nblk = FOURIER_WIDTH // LANES
    return pl.pallas_call(
        _fourier_kernel,
        grid=(BATCH, nblk),
        in_specs=[pl.BlockSpec((1, SEQ_ROWS, LANES), lambda b, c: (b, 0, c)),
                  pl.BlockSpec((1, SEQ_ROWS, LANES), lambda b, c: (b, 0, nblk + c)),
                  _const_spec(m1.shape), _const_spec(m2r.shape),
                  _const_spec(twc.shape), _const_spec(tws.shape)],
        out_specs=pl.BlockSpec((1, FROWS, LANES), lambda b, c: (b, 0, c)),
        out_shape=jax.ShapeDtypeStruct((BATCH, FROWS, FOURIER_WIDTH), F32),
        scratch_shapes=[pltpu.VMEM((N1F * SLAB, LANES), F32)],
        compiler_params=_params(('parallel', 'parallel')),
        name='fourier_fft',
    )(p, p, m1, m2r, twc, tws)


FILT_ROWS = 512


def _filter_kernel(feat_ref, w0_ref, b0_ref, wi_ref, bi_ref, fr_ref, wf_ref, wb_ref, dl_ref,
                   m1_ref, m2_ref, twc_ref, tws_ref, o_ref, h_ref, k_ref, a_ref):
    hi = lax.Precision.HIGHEST
    nchunk = 2 * SEQ // FILT_ROWS

    @pl.when(pl.program_id(0) == 0)
    def _():
        fr = fr_ref[...]

        def mlp(i, carry):
            r0 = pl.multiple_of(i * FILT_ROWS, FILT_ROWS)
            h = jnp.sin(fr * (jnp.dot(feat_ref[pl.ds(r0, FILT_ROWS), :], w0_ref[...],
                                      preferred_element_type=F32, precision=hi) + b0_ref[...]))
            for j in range(FILTER_INNER_LAYERS):
                h = jnp.sin(fr * (jnp.dot(h, wi_ref[j], preferred_element_type=F32, precision=hi)
                                  + bi_ref[j]))
            h_ref[pl.ds(r0, FILT_ROWS), :] = h
            return carry

        lax.fori_loop(0, nchunk, mlp, 0)

    def taps(i, ss):
        r0 = pl.multiple_of(i * FILT_ROWS, FILT_ROWS)
        r = r0 + lax.broadcasted_iota(jnp.int32, (FILT_ROWS, LANES), 0)
        p = (r % N1C) * N2 + r // N1C
        pos = jnp.where(p <= SEQ, p, 2 * SEQ - p)
        t = pos.astype(F32) / F32(SEQ - 1)
        decay = jnp.exp(-t * jnp.abs(dl_ref[...]))
        h = h_ref[pl.ds(r0, FILT_ROWS), :]
        kf = jnp.dot(h, wf_ref[...], preferred_element_type=F32, precision=hi)
        kb = jnp.dot(h, wb_ref[...], preferred_element_type=F32, precision=hi)
        k = jnp.where(p < SEQ, kf, jnp.where(p == SEQ, 0.0, kb)) + jnp.where(p == 0, kb, 0.0)
        k = k * decay
        k_ref[pl.ds(r0, FILT_ROWS), :] = k
        return ss + jnp.sum(k * k, axis=0, keepdims=True)

    sumsq = lax.fori_loop(0, nchunk, taps, jnp.zeros((1, LANES), F32))
    scale = lax.rsqrt(sumsq + EPS) * F32(1.0 / (2 * SEQ))

    def stage1(j, carry):
        n2 = 2 * j
        off0 = pl.multiple_of(n2 * N1C, N1C)
        off1 = pl.multiple_of((n2 + 1) * N1C, N1C)
        x = jnp.concatenate([k_ref[pl.ds(off0, N1C), :], k_ref[pl.ds(off1, N1C), :]], axis=1)
        a = jnp.dot(m1_ref[...], x.astype(BF16), preferred_element_type=F32)
        _store_stage1(a_ref, a, N1C, n2, twc_ref, tws_ref)
        return carry

    lax.fori_loop(0, N2 // 2, stage1, 0, unroll=STAGE1_UNROLL)

    def stage2(j, carry):
        k1 = 2 * j
        x, _, _ = _load_slab_pair(a_ref, k1)
        y = jnp.dot(m2_ref[...], x.astype(BF16), preferred_element_type=F32)
        o0 = pl.multiple_of(k1 * 2 * N2, 2 * N2)
        o_ref[0, pl.ds(o0, 2 * N2), :] = (y[:, :LANES] * scale).astype(BF16)
        o_ref[0, pl.ds(o0 + 2 * N2, 2 * N2), :] = (y[:, LANES:] * scale).astype(BF16)
        return carry

    lax.fori_loop(0, N1C // 2, stage2, 0, unroll=STAGE2_UNROLL)


def _filters(feat, w0, b0, wi, bi, fr, w_out, deltas, m1k, m2, twc, tws):
    nblk = HYENA_WIDTH // LANES
    rows = 2 * SEQ
    return pl.pallas_call(
        _filter_kernel,
        grid=(HYENA_ORDER * nblk,),
        in_specs=[_const_spec(feat.shape), _const_spec(w0.shape), _const_spec(b0.shape),
                  _const_spec(wi.shape), _const_spec(bi.shape), _const_spec(fr.shape),
                  pl.BlockSpec((FILTER_HIDDEN, LANES), lambda g: (0, (g // nblk) * 2 * nblk + g % nblk)),
                  pl.BlockSpec((FILTER_HIDDEN, LANES), lambda g: (0, (g // nblk) * 2 * nblk + nblk + g % nblk)),
                  pl.BlockSpec((1, LANES), lambda g: (0, g % nblk)),
                  _const_spec(m1k.shape), _const_spec(m2.shape),
                  _const_spec(twc.shape), _const_spec(tws.shape)],
        out_specs=pl.BlockSpec((1, N1C * 2 * N2, LANES), lambda g: (g // nblk, 0, g % nblk)),
        out_shape=jax.ShapeDtypeStruct((HYENA_ORDER, N1C * 2 * N2, HYENA_WIDTH), BF16),
        scratch_shapes=[pltpu.VMEM((rows, FILTER_HIDDEN), F32),
                        pltpu.VMEM((rows, LANES), F32),
                        pltpu.VMEM((N1C * SLAB, LANES), F32)],
        compiler_params=_params(('arbitrary',)),
        name='hyena_filters',
    )(feat, w0, b0, wi, bi, fr, w_out, w_out, deltas, m1k, m2, twc, tws)


def _conv_kernel(*refs, conv_u):
    if conv_u:
        (u_ref, g_ref, kf_ref, d_ref, uw_ref, ub_ref, gw_ref, gb_ref, m1_ref, m2_ref, m2i_ref,
         m1i_ref, twc_ref, tws_ref, o_ref, a_ref) = refs
    else:
        (u_ref, g_ref, kf_ref, d_ref, gw_ref, gb_ref, m1_ref, m2_ref, m2i_ref,
         m1i_ref, twc_ref, tws_ref, o_ref, a_ref) = refs

    def u_pair(b, j):
        if conv_u:
            return _shortconv_pair(u_ref, b, j, uw_ref, ub_ref)
        return _rows(u_ref, b, 2 * j), _rows(u_ref, b, 2 * j + 1)

    def stage1(j):
        n2 = 2 * j
        u0 = u_pair(0, j)
        u1 = u_pair(1, j)
        x = jnp.concatenate([jnp.concatenate([u0[h], u1[h]], axis=0) for h in range(2)],
                            axis=1).astype(BF16)
        a = jnp.dot(m1_ref[...], x, preferred_element_type=F32)
        _store_stage1(a_ref, a, N1C, n2, twc_ref, tws_ref)

    _pair_loop(stage1)

    def stage2(j, carry):
        k1 = 2 * j
        x, off0, off1 = _load_slab_pair(a_ref, k1)
        s = jnp.dot(m2_ref[...], x.astype(BF16), preferred_element_type=F32)
        ko = pl.multiple_of(k1 * 2 * N2, 2 * N2)
        kf = kf_ref[0, pl.ds(ko, 4 * N2), :].astype(F32)
        kr = jnp.concatenate([kf[0:N2], kf[2 * N2:3 * N2]], axis=1)
        ki = jnp.concatenate([kf[N2:2 * N2], kf[3 * N2:4 * N2]], axis=1)
        xr = s[:N2]
        xi = s[N2:]
        y = jnp.concatenate([xr * kr - xi * ki, xr * ki + xi * kr], axis=0).astype(BF16)
        b = jnp.dot(m2i_ref[...], y, preferred_element_type=F32)
        a_ref[pl.ds(off0, 2 * N2), :] = b[:, :LANES]
        a_ref[pl.ds(off1, 2 * N2), :] = b[:, LANES:]
        return carry

    lax.fori_loop(0, N1C // 2, stage2, 0, unroll=STAGE2_UNROLL)

    d = d_ref[0]

    def stage3(j):
        t2 = 2 * j
        cols = []
        for h in range(2):
            br = a_ref[pl.ds(t2 + h, N1C, stride=SLAB), :]
            bi = a_ref[pl.ds(N2 + t2 + h, N1C, stride=SLAB), :]
            row0 = _aligned((t2 + h) * N1C, N1C)
            c = twc_ref[pl.ds(row0, N1C), :]
            s = tws_ref[pl.ds(row0, N1C), :]
            cols.append(jnp.concatenate([br * c - bi * s, bi * c + br * s], axis=0))
        x = jnp.concatenate(cols, axis=1).astype(BF16)
        y = jnp.dot(m1i_ref[...], x, preferred_element_type=F32)
        for b in range(2):
            uu = u_pair(b, j)
            gg = _shortconv_pair(g_ref, b, j, gw_ref, gb_ref)
            for h in range(2):
                yy = y[b * CHUNKS:(b + 1) * CHUNKS, h * LANES:(h + 1) * LANES]
                o_ref[b, pl.ds(t2 + h, CHUNKS, stride=CHUNK_PITCH), :] = gg[h] * (yy + d * uu[h])

    _pair_loop(stage3)
    for b in range(2):
        for i in range(CHUNK_PITCH - N2):
            o_ref[b, pl.ds(N2 + i, CHUNKS, stride=CHUNK_PITCH), :] = jnp.zeros((CHUNKS, LANES), F32)


def _longconv(u_arr, u_blk, conv_u, g_arr, g_blk, kf, order, long_d, conv_w, conv_b,
              m1c, m2, m2i, m1i, twc, tws):
    nblk = HYENA_WIDTH // LANES
    hy_first = 2 * FOURIER_WIDTH // LANES
    blk = (2, SEQ_ROWS, LANES)
    args = [u_arr, g_arr, kf, long_d.reshape(HYENA_ORDER, 1, HYENA_WIDTH)]
    specs = [pl.BlockSpec(blk, lambda c, b: (b, 0, u_blk + c)),
             pl.BlockSpec(blk, lambda c, b: (b, 0, g_blk + c)),
             pl.BlockSpec((1, N1C * 2 * N2, LANES), lambda c, b: (order, 0, c)),
             pl.BlockSpec((1, 1, LANES), lambda c, b: (order, 0, c))]
    for blk0 in ([u_blk] if conv_u else []) + [g_blk]:
        args += [conv_w, conv_b.reshape(1, -1)]
        specs += [pl.BlockSpec((3, LANES), lambda c, b, o=blk0 - hy_first: (0, o + c)),
                  pl.BlockSpec((1, LANES), lambda c, b, o=blk0 - hy_first: (0, o + c))]
    consts = (m1c, m2, m2i, m1i, twc, tws)
    return pl.pallas_call(
        functools.partial(_conv_kernel, conv_u=conv_u),
        grid=(nblk, BATCH // 2),
        in_specs=specs + [_const_spec(c.shape) for c in consts],
        out_specs=pl.BlockSpec(blk, lambda c, b: (b, 0, c)),
        out_shape=jax.ShapeDtypeStruct((BATCH, SEQ_ROWS, HYENA_WIDTH), F32),
        scratch_shapes=[pltpu.VMEM((N1C * SLAB, LANES), F32)],
        compiler_params=_params(('parallel', 'parallel')),
        name=f'hyena_conv{order}',
    )(*args, *consts)


FF_CHUNK = 1024


def _tail_kernel(x_ref, yf_ref, yh_ref, gf_ref, gh_ref, wo_ref, gm_ref, w1_ref, w2_ref, gl_ref, o_ref):
    def norm(v, g):
        return v * lax.rsqrt(jnp.mean(v * v, axis=-1, keepdims=True) + EPS) * g

    yf = jnp.concatenate([yf_ref[0, j * FPITCH:j * FPITCH + N1F, :]
                          for j in range(ROW_TILE // N1F)], axis=0)
    yh = jnp.concatenate([yh_ref[0, j * CHUNK_PITCH:j * CHUNK_PITCH + N2, :]
                          for j in range(TILE_CHUNKS)], axis=0)
    y = jnp.concatenate([norm(yf, gf_ref[...]), norm(yh, gh_ref[...])], axis=1).astype(BF16)
    xm = x_ref[0] + jnp.dot(y, wo_ref[...], preferred_element_type=F32)
    hm = norm(xm, gm_ref[...]).astype(BF16)
    acc = xm
    for c in range(D_FF // FF_CHUNK):
        a = jnp.dot(hm, w1_ref[:, c * FF_CHUNK:(c + 1) * FF_CHUNK], preferred_element_type=F32)
        a = jnp.square(jnp.maximum(a, 0.0)).astype(BF16)
        acc = acc + jnp.dot(a, w2_ref[c * FF_CHUNK:(c + 1) * FF_CHUNK, :], preferred_element_type=F32)
    o_ref[0] = norm(acc, gl_ref[...])


def _tail(x, yf, yh, g_f, g_h, wo, g_mlp, w1, w2, g_final):
    row = lambda v: v.reshape(1, -1)
    return pl.pallas_call(
        _tail_kernel,
        grid=(BATCH, SEQ // ROW_TILE),
        in_specs=[pl.BlockSpec((1, ROW_TILE, D_MODEL), lambda b, i: (b, i, 0)),
                  pl.BlockSpec((1, ROW_TILE // N1F * FPITCH, FOURIER_WIDTH), lambda b, i: (b, i, 0)),
                  pl.BlockSpec((1, TILE_CHUNKS * CHUNK_PITCH, HYENA_WIDTH), lambda b, i: (b, i, 0)),
                  _const_spec((1, FOURIER_WIDTH)), _const_spec((1, HYENA_WIDTH)),
                  _const_spec((D_MODEL, D_MODEL)), _const_spec((1, D_MODEL)),
                  _const_spec((D_MODEL, D_FF)), _const_spec((D_FF, D_MODEL)),
                  _const_spec((1, D_MODEL))],
        out_specs=pl.BlockSpec((1, ROW_TILE, D_MODEL), lambda b, i: (b, i, 0)),
        out_shape=jax.ShapeDtypeStruct((BATCH, SEQ, D_MODEL), F32),
        compiler_params=_params(('parallel', 'parallel')),
        name='out_proj_mlp',
    )(x, yf, yh, row(g_f), row(g_h), wo, row(g_mlp), w1, w2, row(g_final))


def _filter_features():
    r = np.arange(2 * SEQ)
    p = (r % N1C) * N2 + r // N1C
    pos = np.where(p <= SEQ, p, 2 * SEQ - p).astype(np.float64)
    t = pos / (SEQ - 1)
    bands = np.linspace(1e-4, POS_BANDS - 1, POS_BANDS)
    ang = (2.0 * np.pi * pos / SEQ)[:, None] * bands[None, :]
    z = np.concatenate([t[:, None], np.cos(ang), -np.sin(ang)], axis=-1)
    return jnp.asarray(np.pad(z, ((0, 0), (0, LANES - POS_EMB_DIM))).astype(np.float32))


def kernel(x, g_mix, w_in, conv_w, conv_b, filt_w0, filt_b0, filt_w_inner, filt_b_inner, filt_freq,
           filt_w_out, long_d, g_fourier, g_hyena, w_out, g_mlp, w_fc1, w_fc2, g_final):
    tb = _tables()
    bf = lambda v: v.astype(BF16)
    twc_f, tws_f = _twiddles(N1F, SEQ)
    twc_c, tws_c = _twiddles(N1C, 2 * SEQ)

    w0 = jnp.pad(filt_w0[0], ((0, LANES - POS_EMB_DIM), (0, 0)))
    deltas = jnp.linspace(MIN_DECAY, MAX_DECAY, HYENA_WIDTH, dtype=F32).reshape(1, HYENA_WIDTH)
    kf = _filters(_filter_features(), w0, filt_b0[0].reshape(1, -1), filt_w_inner[0],
                  filt_b_inner[0].reshape(FILTER_INNER_LAYERS, 1, FILTER_HIDDEN),
                  filt_freq[0].reshape(1, -1), filt_w_out[0], deltas,
                  bf(tb['m1k']), bf(tb['m2']), twc_c, tws_c)

    w_prep = _prep_weights(w_in[0], g_mix[0], tb['chan'])
    p = _inproj(x, w_prep)
    yf = _fourier(p, bf(tb['m1f']), bf(tb['m2r']), twc_f, tws_f)
    nblk = HYENA_WIDTH // LANES
    x1_blk = 2 * FOURIER_WIDTH // LANES
    fft = (bf(tb['m1c']), bf(tb['m2']), bf(tb['m2i']), bf(tb['m1i']), twc_c, tws_c)
    z1 = _longconv(p, x1_blk + 2 * nblk, True, p, x1_blk, kf, 0, long_d[0], conv_w[0], conv_b[0], *fft)
    z2 = _longconv(z1, 0, False, p, x1_blk + nblk, kf, 1, long_d[0], conv_w[0], conv_b[0], *fft)

    return _tail(x, yf, z2, g_fourier[0], g_hyena[0], bf(w_out[0]), g_mlp[0],
                 bf(w_fc1[0]), bf(w_fc2[0]), g_final)
```

```python
import functools
import math

import numpy as np
import jax
import jax.numpy as jnp
from jax import lax
from jax.experimental import pallas as pl
from jax.experimental.pallas import tpu as pltpu

F32 = jnp.float32
BF16 = jnp.bfloat16

D_MODEL = 1024
BATCH = 8
SEQ = 4096
FOURIER_WIDTH = 512
FOURIER_GROUP_DIM = 64
HYENA_WIDTH = 512
HYENA_ORDER = 2
POS_BANDS = 16
POS_EMB_DIM = 1 + 2 * POS_BANDS
FILTER_HIDDEN = 64
FILTER_INNER_LAYERS = 2
D_FF = 4 * D_MODEL
DECAY_TARGET = 1e-2
MAX_DECAY = math.log(DECAY_TARGET) / 0.3
MIN_DECAY = math.log(DECAY_TARGET) / 1.5
EPS = 1e-5

LANES = 128
N2 = 128
CHUNKS = SEQ // N2
CHUNK_PITCH = N2 + 8
SEQ_ROWS = CHUNKS * CHUNK_PITCH
N1F = SEQ // N2
N1C = 2 * SEQ // N2
SLAB = 2 * N2 + 8
FPITCH = N1F + 8
FROWS = N2 * FPITCH
IN_COLS = 2 * FOURIER_WIDTH + 3 * HYENA_WIDTH
VMEM_LIMIT = 60 * 1024 * 1024
STAGE1_UNROLL = 8
STAGE2_UNROLL = 8


def _cs(rows, cols, n):
    m = (np.outer(np.arange(rows), np.arange(cols)) % n).astype(np.float64)
    ang = 2.0 * np.pi * m / n
    return np.cos(ang), np.sin(ang)


def _tables():
    t = {}
    c, s = _cs(FOURIER_GROUP_DIM, FOURIER_GROUP_DIM, FOURIER_GROUP_DIM)
    eye = np.eye(FOURIER_WIDTH // FOURIER_GROUP_DIM)
    scale = 1.0 / math.sqrt(SEQ * FOURIER_GROUP_DIM)
    t['chan'] = np.concatenate([np.kron(eye, c), -np.kron(eye, s)], axis=1) * scale
    c, s = _cs(N1F, N1F, N1F)
    t['m1f'] = np.block([[c, s], [-s, c]])
    c, s = _cs(N1C, CHUNKS, N1C)
    t['m1c'] = np.block([[c, s], [-s, c]])
    c, s = _cs(N1C, N1C, N1C)
    order = list(range(CHUNKS)) + list(range(N1C - 1, CHUNKS - 1, -1))
    t['m1k'] = np.concatenate([c[:, order], -s[:, order]], axis=0)
    c, s = _cs(CHUNKS, N1C, N1C)
    t['m1i'] = np.block([[c, -s], [s, c]])
    c, s = _cs(N2, N2, N2)
    t['m2'] = np.block([[c, s], [-s, c]])
    t['m2i'] = np.block([[c, -s], [s, c]])
    t['m2r'] = np.concatenate([c, s], axis=1)
    return {k: jnp.asarray(v.astype(np.float32)) for k, v in t.items()}


def _twiddles(n1, n):
    r = np.arange(N2 * n1)
    ang = 2.0 * np.pi * ((r // n1) * (r % n1)).astype(np.float64) / n
    wide = lambda v: jnp.asarray(np.broadcast_to(v.astype(np.float32)[:, None], (N2 * n1, LANES)))
    return wide(np.cos(ang)), wide(np.sin(ang))


def _const_spec(shape):
    nd = len(shape)
    return pl.BlockSpec(shape, lambda *_: (0,) * nd, pipeline_mode=pl.Buffered(1))


def _params(sem):
    return pltpu.CompilerParams(dimension_semantics=sem, vmem_limit_bytes=VMEM_LIMIT)


def _prep_kernel(w_ref, g_ref, chan_ref, o_ref):
    g = g_ref[...]
    wf = w_ref[:, :FOURIER_WIDTH]
    z = jnp.dot(wf, chan_ref[...], preferred_element_type=F32,
                precision=lax.Precision.HIGHEST)
    o_ref[:, :2 * FOURIER_WIDTH] = (g * z).astype(BF16)
    o_ref[:, 2 * FOURIER_WIDTH:] = (g * w_ref[:, FOURIER_WIDTH:]).astype(BF16)


def _prep_weights(w_in, g_mix, chan):
    return pl.pallas_call(
        _prep_kernel,
        out_shape=jax.ShapeDtypeStruct((D_MODEL, IN_COLS), BF16),
        compiler_params=_params(None),
        name='prep_weights',
    )(w_in, g_mix.reshape(D_MODEL, 1), chan)


ROW_TILE = 512
TILE_CHUNKS = ROW_TILE // N2


def _inproj_kernel(x_ref, w_ref, o_ref):
    x = x_ref[0]
    r = lax.rsqrt(jnp.mean(x * x, axis=-1, keepdims=True) + EPS)
    h = (x * r).astype(BF16)
    p = jnp.dot(h, w_ref[...], preferred_element_type=F32)
    for j in range(TILE_CHUNKS):
        o_ref[0, j * CHUNK_PITCH:j * CHUNK_PITCH + N2, :] = p[j * N2:(j + 1) * N2]
        o_ref[0, j * CHUNK_PITCH + N2:(j + 1) * CHUNK_PITCH, :] = jnp.zeros(
            (CHUNK_PITCH - N2, IN_COLS), F32)


def _inproj(x, w):
    return pl.pallas_call(
        _inproj_kernel,
        grid=(BATCH, SEQ // ROW_TILE),
        in_specs=[pl.BlockSpec((1, ROW_TILE, D_MODEL), lambda b, i: (b, i, 0)),
                  _const_spec((D_MODEL, IN_COLS))],
        out_specs=pl.BlockSpec((1, TILE_CHUNKS * CHUNK_PITCH, IN_COLS), lambda b, i: (b, i, 0)),
        out_shape=jax.ShapeDtypeStruct((BATCH, SEQ_ROWS, IN_COLS), F32),
        compiler_params=_params(('parallel', 'parallel')),
        name='in_proj',
    )(x, w)


def _rows(ref, b, n2):
    return ref[b, pl.ds(n2, CHUNKS, stride=CHUNK_PITCH), :]


def _pair_loop(body):
    peeled = N2 // 2 - 1 - (N2 // 2 - 2) // STAGE1_UNROLL * STAGE1_UNROLL
    body(0)
    lax.fori_loop(1, N2 // 2 - peeled, lambda j, c: (body(j), c)[1], 0, unroll=STAGE1_UNROLL)
    for j in range(N2 // 2 - peeled, N2 // 2):
        body(j)


def _shortconv_pair(ref, b, j, w_ref, b_ref):
    n2 = 2 * j
    row = lax.broadcasted_iota(jnp.int32, (CHUNKS, LANES), 0)
    c0 = _rows(ref, b, n2)
    c1 = _rows(ref, b, n2 + 1)
    if isinstance(j, int) and j == 0:
        before = jnp.where(row == 0, 0.0, pltpu.roll(_rows(ref, b, N2 - 1), 1, axis=0))
    else:
        before = _rows(ref, b, n2 - 1)
    if isinstance(j, int) and j == N2 // 2 - 1:
        after = jnp.where(row == CHUNKS - 1, 0.0, pltpu.roll(_rows(ref, b, 0), CHUNKS - 1, axis=0))
    else:
        after = _rows(ref, b, n2 + 2)
    w0 = w_ref[0:1, :]
    w1 = w_ref[1:2, :]
    w2 = w_ref[2:3, :]
    bias = b_ref[...]
    return (before * w0 + c0 * w1 + c1 * w2 + bias,
            c0 * w0 + c1 * w1 + after * w2 + bias)


def _aligned(v, m):
    return v if isinstance(v, int) else pl.multiple_of(v, m)


def _store_stage1(a_ref, a, n1, n2, twc_ref, tws_ref):
    for h in range(2):
        ar = a[:n1, h * LANES:(h + 1) * LANES]
        ai = a[n1:, h * LANES:(h + 1) * LANES]
        row0 = _aligned((n2 + h) * n1, n1)
        c = twc_ref[pl.ds(row0, n1), :]
        s = tws_ref[pl.ds(row0, n1), :]
        a_ref[pl.ds(n2 + h, n1, stride=SLAB), :] = ar * c + ai * s
        a_ref[pl.ds(N2 + n2 + h, n1, stride=SLAB), :] = ai * c - ar * s


def _load_slab_pair(a_ref, k1):
    off0 = pl.multiple_of(k1 * SLAB, 8)
    off1 = pl.multiple_of((k1 + 1) * SLAB, 8)
    x = jnp.concatenate([a_ref[pl.ds(off0, 2 * N2), :], a_ref[pl.ds(off1, 2 * N2), :]], axis=1)
    return x, off0, off1


def _fourier_kernel(zr_ref, zi_ref, m1_ref, m2_ref, twc_ref, tws_ref, o_ref, a_ref):
    def stage1(j, carry):
        n2 = 2 * j
        cols = []
        for h in range(2):
            cols.append(jnp.concatenate(
                [zr_ref[0, pl.ds(n2 + h, N1F, stride=CHUNK_PITCH), :],
                 zi_ref[0, pl.ds(n2 + h, N1F, stride=CHUNK_PITCH), :]], axis=0))
        x = jnp.concatenate(cols, axis=1).astype(BF16)
        a = jnp.dot(m1_ref[...], x, preferred_element_type=F32)
        _store_stage1(a_ref, a, N1F, n2, twc_ref, tws_ref)
        return carry

    lax.fori_loop(0, N2 // 2, stage1, 0, unroll=STAGE1_UNROLL)

    def stage2(j, carry):
        k1 = 2 * j
        x, _, _ = _load_slab_pair(a_ref, k1)
        y = jnp.dot(m2_ref[...], x.astype(BF16), preferred_element_type=F32)
        o_ref[0, pl.ds(k1, N2, stride=FPITCH), :] = y[:, :LANES]
        o_ref[0, pl.ds(k1 + 1, N2, stride=FPITCH), :] = y[:, LANES:]
        return carry

    lax.fori_loop(0, N1F // 2, stage2, 0, unroll=STAGE2_UNROLL)
    for i in range(FPITCH - N1F):
        o_ref[0, pl.ds(N1F + i, N2, stride=FPITCH), :] = jnp.zeros((N2, LANES), F32)


def _fourier(p, m1, m2r, twc, tws):
    nblk = FOURIER_WIDTH // LANES
    return pl.pallas_call(
        _fourier_kernel,
        grid=(BATCH, nblk),
        in_specs=[pl.BlockSpec((1, SEQ_ROWS, LANES), lambda b, c: (b, 0, c)),
                  pl.BlockSpec((1, SEQ_ROWS, LANES), lambda b, c: (b, 0, nblk + c)),
                  _const_spec(m1.shape), _const_spec(m2r.shape),
                  _const_spec(twc.shape), _const_spec(tws.shape)],
        out_specs=pl.BlockSpec((1, FROWS, LANES), lambda b, c: (b, 0, c)),
        out_shape=jax.ShapeDtypeStruct((BATCH, FROWS, FOURIER_WIDTH), F32),
        scratch_shapes=[pltpu.VMEM((N1F * SLAB, LANES), F32)],
        compiler_params=_params(('parallel', 'parallel')),
        name='fourier_fft',
    )(p, p, m1, m2r, twc, tws)


FILT_ROWS = 512


def _filter_kernel(feat_ref, w0_ref, b0_ref, wi_ref, bi_ref, fr_ref, wf_ref, wb_ref, dl_ref,
                   m1_ref, m2_ref, twc_ref, tws_ref, o_ref, h_ref, k_ref, a_ref):
    hi = lax.Precision.HIGHEST
    nchunk = SEQ // FILT_ROWS
    blocks = FILT_ROWS // CHUNKS

    @pl.when(pl.program_id(0) == 0)
    def _():
        fr = fr_ref[...]

        def mlp(i, carry):
            r0 = pl.multiple_of(i * FILT_ROWS, FILT_ROWS)
            h = jnp.sin(fr * (jnp.dot(feat_ref[pl.ds(r0, FILT_ROWS), :], w0_ref[...],
                                      preferred_element_type=F32, precision=hi) + b0_ref[...]))
            for j in range(FILTER_INNER_LAYERS):
                h = jnp.sin(fr * (jnp.dot(h, wi_ref[j], preferred_element_type=F32, precision=hi)
                                  + bi_ref[j]))
            h_ref[pl.ds(r0, FILT_ROWS), :] = h
            return carry

        lax.fori_loop(0, nchunk, mlp, 0)

    w_both = jnp.concatenate([wf_ref[...], wb_ref[...]], axis=1)
    dl = jnp.abs(dl_ref[...])

    def taps(i, carry):
        r0 = pl.multiple_of(i * FILT_ROWS, FILT_ROWS)
        r = r0 + lax.broadcasted_iota(jnp.int32, (FILT_ROWS, LANES), 0)
        pos = (r % CHUNKS) * N2 + r // CHUNKS
        decay = jnp.exp(-(pos.astype(F32) / F32(SEQ - 1)) * dl)
        k = jnp.dot(h_ref[pl.ds(r0, FILT_ROWS), :], w_both, preferred_element_type=F32, precision=hi)
        kf = k[:, :LANES] * decay
        kb = k[:, LANES:] * decay
        for q in range(blocks):
            n2 = i * blocks + q
            k_ref[pl.ds(pl.multiple_of(n2 * N1C, N1C), CHUNKS), :] = kf[q * CHUNKS:(q + 1) * CHUNKS]
            dst = ((N2 - n2) % N2) * N1C + CHUNKS
            k_ref[pl.ds(pl.multiple_of(dst, CHUNKS), CHUNKS), :] = kb[q * CHUNKS:(q + 1) * CHUNKS]
        return carry

    lax.fori_loop(0, nchunk, taps, 0)
    row = lax.broadcasted_iota(jnp.int32, (CHUNKS, LANES), 0)
    kb0 = k_ref[CHUNKS:2 * CHUNKS, :]
    k_ref[0:CHUNKS, :] = k_ref[0:CHUNKS, :] + jnp.where(row == 0, kb0, 0.0)
    k_ref[CHUNKS:2 * CHUNKS, :] = jnp.where(row == CHUNKS - 1, 0.0, pltpu.roll(kb0, CHUNKS - 1, axis=0))

    def sq(i, ss):
        k = k_ref[pl.ds(pl.multiple_of(i * FILT_ROWS, FILT_ROWS), FILT_ROWS), :]
        return ss + jnp.sum(k * k, axis=0, keepdims=True)

    sumsq = lax.fori_loop(0, 2 * nchunk, sq, jnp.zeros((1, LANES), F32))
    scale = lax.rsqrt(sumsq + EPS) * F32(1.0 / (2 * SEQ))

    def stage1(j, carry):
        n2 = 2 * j
        off0 = pl.multiple_of(n2 * N1C, N1C)
        off1 = pl.multiple_of((n2 + 1) * N1C, N1C)
        x = jnp.concatenate([k_ref[pl.ds(off0, N1C), :], k_ref[pl.ds(off1, N1C), :]], axis=1)
        a = jnp.dot(m1_ref[...], x.astype(BF16), preferred_element_type=F32)
        _store_stage1(a_ref, a, N1C, n2, twc_ref, tws_ref)
        return carry

    lax.fori_loop(0, N2 // 2, stage1, 0, unroll=STAGE1_UNROLL)

    def stage2(j, carry):
        k1 = 2 * j
        x, _, _ = _load_slab_pair(a_ref, k1)
        y = jnp.dot(m2_ref[...], x.astype(BF16), preferred_element_type=F32)
        o0 = pl.multiple_of(k1 * 2 * N2, 2 * N2)
        o_ref[0, pl.ds(o0, 2 * N2), :] = (y[:, :LANES] * scale).astype(BF16)
        o_ref[0, pl.ds(o0 + 2 * N2, 2 * N2), :] = (y[:, LANES:] * scale).astype(BF16)
        return carry

    lax.fori_loop(0, N1C // 2, stage2, 0, unroll=STAGE2_UNROLL)


def _filters(feat, w0, b0, wi, bi, fr, w_out, deltas, m1k, m2, twc, tws):
    nblk = HYENA_WIDTH // LANES
    rows = 2 * SEQ
    return pl.pallas_call(
        _filter_kernel,
        grid=(HYENA_ORDER * nblk,),
        in_specs=[_const_spec(feat.shape), _const_spec(w0.shape), _const_spec(b0.shape),
                  _const_spec(wi.shape), _const_spec(bi.shape), _const_spec(fr.shape),
                  pl.BlockSpec((FILTER_HIDDEN, LANES), lambda g: (0, (g // nblk) * 2 * nblk + g % nblk)),
                  pl.BlockSpec((FILTER_HIDDEN, LANES), lambda g: (0, (g // nblk) * 2 * nblk + nblk + g % nblk)),
                  pl.BlockSpec((1, LANES), lambda g: (0, g % nblk)),
                  _const_spec(m1k.shape), _const_spec(m2.shape),
                  _const_spec(twc.shape), _const_spec(tws.shape)],
        out_specs=pl.BlockSpec((1, N1C * 2 * N2, LANES), lambda g: (g // nblk, 0, g % nblk)),
        out_shape=jax.ShapeDtypeStruct((HYENA_ORDER, N1C * 2 * N2, HYENA_WIDTH), BF16),
        scratch_shapes=[pltpu.VMEM((SEQ, FILTER_HIDDEN), F32),
                        pltpu.VMEM((rows, LANES), F32),
                        pltpu.VMEM((N1C * SLAB, LANES), F32)],
        compiler_params=_params(('arbitrary',)),
        name='hyena_filters',
    )(feat, w0, b0, wi, bi, fr, w_out, w_out, deltas, m1k, m2, twc, tws)


def _conv_kernel(*refs, conv_u):
    if conv_u:
        (u_ref, g_ref, kf_ref, d_ref, uw_ref, ub_ref, gw_ref, gb_ref, m1_ref, m2_ref, m2i_ref,
         m1i_ref, twc_ref, tws_ref, o_ref, a_ref) = refs
    else:
        (u_ref, g_ref, kf_ref, d_ref, gw_ref, gb_ref, m1_ref, m2_ref, m2i_ref,
         m1i_ref, twc_ref, tws_ref, o_ref, a_ref) = refs

    def u_pair(b, j):
        if conv_u:
            return _shortconv_pair(u_ref, b, j, uw_ref, ub_ref)
        return _rows(u_ref, b, 2 * j), _rows(u_ref, b, 2 * j + 1)

    def stage1(j):
        n2 = 2 * j
        u0 = u_pair(0, j)
        u1 = u_pair(1, j)
        x = jnp.concatenate([jnp.concatenate([u0[h], u1[h]], axis=0) for h in range(2)],
                            axis=1).astype(BF16)
        a = jnp.dot(m1_ref[...], x, preferred_element_type=F32)
        _store_stage1(a_ref, a, N1C, n2, twc_ref, tws_ref)

    _pair_loop(stage1)

    def stage2(j, carry):
        k1 = 2 * j
        x, off0, off1 = _load_slab_pair(a_ref, k1)
        s = jnp.dot(m2_ref[...], x.astype(BF16), preferred_element_type=F32)
        ko = pl.multiple_of(k1 * 2 * N2, 2 * N2)
        kf = kf_ref[0, pl.ds(ko, 4 * N2), :].astype(F32)
        kr = jnp.concatenate([kf[0:N2], kf[2 * N2:3 * N2]], axis=1)
        ki = jnp.concatenate([kf[N2:2 * N2], kf[3 * N2:4 * N2]], axis=1)
        xr = s[:N2]
        xi = s[N2:]
        y = jnp.concatenate([xr * kr - xi * ki, xr * ki + xi * kr], axis=0).astype(BF16)
        b = jnp.dot(m2i_ref[...], y, preferred_element_type=F32)
        a_ref[pl.ds(off0, 2 * N2), :] = b[:, :LANES]
        a_ref[pl.ds(off1, 2 * N2), :] = b[:, LANES:]
        return carry

    lax.fori_loop(0, N1C // 2, stage2, 0, unroll=STAGE2_UNROLL)

    d = d_ref[0]

    def stage3(j):
        t2 = 2 * j
        cols = []
        for h in range(2):
            br = a_ref[pl.ds(t2 + h, N1C, stride=SLAB), :]
            bi = a_ref[pl.ds(N2 + t2 + h, N1C, stride=SLAB), :]
            row0 = _aligned((t2 + h) * N1C, N1C)
            c = twc_ref[pl.ds(row0, N1C), :]
            s = tws_ref[pl.ds(row0, N1C), :]
            cols.append(jnp.concatenate([br * c - bi * s, bi * c + br * s], axis=0))
        x = jnp.concatenate(cols, axis=1).astype(BF16)
        y = jnp.dot(m1i_ref[...], x, preferred_element_type=F32)
        for b in range(2):
            uu = u_pair(b, j)
            gg = _shortconv_pair(g_ref, b, j, gw_ref, gb_ref)
            for h in range(2):
                yy = y[b * CHUNKS:(b + 1) * CHUNKS, h * LANES:(h + 1) * LANES]
                o_ref[b, pl.ds(t2 + h, CHUNKS, stride=CHUNK_PITCH), :] = gg[h] * (yy + d * uu[h])

    _pair_loop(stage3)
    for b in range(2):
        for i in range(CHUNK_PITCH - N2):
            o_ref[b, pl.ds(N2 + i, CHUNKS, stride=CHUNK_PITCH), :] = jnp.zeros((CHUNKS, LANES), F32)


def _longconv(u_arr, u_blk, conv_u, g_arr, g_blk, kf, order, long_d, conv_w, conv_b,
              m1c, m2, m2i, m1i, twc, tws):
    nblk = HYENA_WIDTH // LANES
    hy_first = 2 * FOURIER_WIDTH // LANES
    blk = (2, SEQ_ROWS, LANES)
    args = [u_arr, g_arr, kf, long_d.reshape(HYENA_ORDER, 1, HYENA_WIDTH)]
    specs = [pl.BlockSpec(blk, lambda c, b: (b, 0, u_blk + c)),
             pl.BlockSpec(blk, lambda c, b: (b, 0, g_blk + c)),
             pl.BlockSpec((1, N1C * 2 * N2, LANES), lambda c, b: (order, 0, c)),
             pl.BlockSpec((1, 1, LANES), lambda c, b: (order, 0, c))]
    for blk0 in ([u_blk] if conv_u else []) + [g_blk]:
        args += [conv_w, conv_b.reshape(1, -1)]
        specs += [pl.BlockSpec((3, LANES), lambda c, b, o=blk0 - hy_first: (0, o + c)),
                  pl.BlockSpec((1, LANES), lambda c, b, o=blk0 - hy_first: (0, o + c))]
    consts = (m1c, m2, m2i, m1i, twc, tws)
    return pl.pallas_call(
        functools.partial(_conv_kernel, conv_u=conv_u),
        grid=(nblk, BATCH // 2),
        in_specs=specs + [_const_spec(c.shape) for c in consts],
        out_specs=pl.BlockSpec(blk, lambda c, b: (b, 0, c)),
        out_shape=jax.ShapeDtypeStruct((BATCH, SEQ_ROWS, HYENA_WIDTH), F32),
        scratch_shapes=[pltpu.VMEM((N1C * SLAB, LANES), F32)],
        compiler_params=_params(('parallel', 'parallel')),
        name=f'hyena_conv{order}',
    )(*args, *consts)


FF_CHUNK = 1024


def _tail_kernel(x_ref, yf_ref, yh_ref, gf_ref, gh_ref, wo_ref, gm_ref, w1_ref, w2_ref, gl_ref, o_ref):
    def norm(v, g):
        return v * lax.rsqrt(jnp.mean(v * v, axis=-1, keepdims=True) + EPS) * g

    yf = jnp.concatenate([yf_ref[0, j * FPITCH:j * FPITCH + N1F, :]
                          for j in range(ROW_TILE // N1F)], axis=0)
    yh = jnp.concatenate([yh_ref[0, j * CHUNK_PITCH:j * CHUNK_PITCH + N2, :]
                          for j in range(TILE_CHUNKS)], axis=0)
    y = jnp.concatenate([norm(yf, gf_ref[...]), norm(yh, gh_ref[...])], axis=1).astype(BF16)
    xm = x_ref[0] + jnp.dot(y, wo_ref[...], preferred_element_type=F32)
    hm = norm(xm, gm_ref[...]).astype(BF16)
    acc = xm
    for c in range(D_FF // FF_CHUNK):
        a = jnp.dot(hm, w1_ref[:, c * FF_CHUNK:(c + 1) * FF_CHUNK], preferred_element_type=F32)
        a = jnp.square(jnp.maximum(a, 0.0)).astype(BF16)
        acc = acc + jnp.dot(a, w2_ref[c * FF_CHUNK:(c + 1) * FF_CHUNK, :], preferred_element_type=F32)
    o_ref[0] = norm(acc, gl_ref[...])


def _tail(x, yf, yh, g_f, g_h, wo, g_mlp, w1, w2, g_final):
    row = lambda v: v.reshape(1, -1)
    return pl.pallas_call(
        _tail_kernel,
        grid=(BATCH, SEQ // ROW_TILE),
        in_specs=[pl.BlockSpec((1, ROW_TILE, D_MODEL), lambda b, i: (b, i, 0)),
                  pl.BlockSpec((1, ROW_TILE // N1F * FPITCH, FOURIER_WIDTH), lambda b, i: (b, i, 0)),
                  pl.BlockSpec((1, TILE_CHUNKS * CHUNK_PITCH, HYENA_WIDTH), lambda b, i: (b, i, 0)),
                  _const_spec((1, FOURIER_WIDTH)), _const_spec((1, HYENA_WIDTH)),
                  _const_spec((D_MODEL, D_MODEL)), _const_spec((1, D_MODEL)),
                  _const_spec((D_MODEL, D_FF)), _const_spec((D_FF, D_MODEL)),
                  _const_spec((1, D_MODEL))],
        out_specs=pl.BlockSpec((1, ROW_TILE, D_MODEL), lambda b, i: (b, i, 0)),
        out_shape=jax.ShapeDtypeStruct((BATCH, SEQ, D_MODEL), F32),
        compiler_params=_params(('parallel', 'parallel')),
        name='out_proj_mlp',
    )(x, yf, yh, row(g_f), row(g_h), wo, row(g_mlp), w1, w2, row(g_final))


def _filter_features():
    r = np.arange(SEQ)
    pos = ((r % CHUNKS) * N2 + r // CHUNKS).astype(np.float64)
    t = pos / (SEQ - 1)
    bands = np.linspace(1e-4, POS_BANDS - 1, POS_BANDS)
    ang = (2.0 * np.pi * pos / SEQ)[:, None] * bands[None, :]
    z = np.concatenate([t[:, None], np.cos(ang), -np.sin(ang)], axis=-1)
    return jnp.asarray(np.pad(z, ((0, 0), (0, LANES - POS_EMB_DIM))).astype(np.float32))


def kernel(x, g_mix, w_in, conv_w, conv_b, filt_w0, filt_b0, filt_w_inner, filt_b_inner, filt_freq,
           filt_w_out, long_d, g_fourier, g_hyena, w_out, g_mlp, w_fc1, w_fc2, g_final):
    tb = _tables()
    bf = lambda v: v.astype(BF16)
    twc_f, tws_f = _twiddles(N1F, SEQ)
    twc_c, tws_c = _twiddles(N1C, 2 * SEQ)

    w0 = jnp.pad(filt_w0[0], ((0, LANES - POS_EMB_DIM), (0, 0)))
    deltas = jnp.linspace(MIN_DECAY, MAX_DECAY, HYENA_WIDTH, dtype=F32).reshape(1, HYENA_WIDTH)
    kf = _filters(_filter_features(), w0, filt_b0[0].reshape(1, -1), filt_w_inner[0],
                  filt_b_inner[0].reshape(FILTER_INNER_LAYERS, 1, FILTER_HIDDEN),
                  filt_freq[0].reshape(1, -1), filt_w_out[0], deltas,
                  bf(tb['m1k']), bf(tb['m2']), twc_c, tws_c)

    w_prep = _prep_weights(w_in[0], g_mix[0], tb['chan'])
    p = _inproj(x, w_prep)
    yf = _fourier(p, bf(tb['m1f']), bf(tb['m2r']), twc_f, tws_f)
    nblk = HYENA_WIDTH // LANES
    x1_blk = 2 * FOURIER_WIDTH // LANES
    fft = (bf(tb['m1c']), bf(tb['m2']), bf(tb['m2i']), bf(tb['m1i']), twc_c, tws_c)
    z1 = _longconv(p, x1_blk + 2 * nblk, True, p, x1_blk, kf, 0, long_d[0], conv_w[0], conv_b[0], *fft)
    z2 = _longconv(z1, 0, False, p, x1_blk + nblk, kf, 1, long_d[0], conv_w[0], conv_b[0], *fft)

    return _tail(x, yf, z2, g_fourier[0], g_hyena[0], bf(w_out[0]), g_mlp[0],
                 bf(w_fc1[0]), bf(w_fc2[0]), g_final)
```

```python
import math

import numpy as np
import jax
import jax.numpy as jnp
from jax import lax
from jax.experimental import pallas as pl
from jax.experimental.pallas import tpu as pltpu

F32 = jnp.float32
BF16 = jnp.bfloat16

D_MODEL = 1024
BATCH = 8
SEQ = 4096
FOURIER_WIDTH = 512
FOURIER_GROUP_DIM = 64
HYENA_WIDTH = 512
HYENA_ORDER = 2
POS_BANDS = 16
POS_EMB_DIM = 1 + 2 * POS_BANDS
FILTER_HIDDEN = 64
FILTER_INNER_LAYERS = 2
D_FF = 4 * D_MODEL
DECAY_TARGET = 1e-2
MAX_DECAY = math.log(DECAY_TARGET) / 0.3
MIN_DECAY = math.log(DECAY_TARGET) / 1.5
EPS = 1e-5

LANES = 128
N2 = 128
CHUNKS = SEQ // N2
CHUNK_PITCH = N2 + 8
SEQ_ROWS = CHUNKS * CHUNK_PITCH
N1F = SEQ // N2
N1C = 2 * SEQ // N2
SLAB = 2 * N2 + 8
FPITCH = N1F + 8
FROWS = N2 * FPITCH
IN_COLS = 2 * FOURIER_WIDTH + 3 * HYENA_WIDTH
VMEM_LIMIT = 60 * 1024 * 1024
STAGE1_UNROLL = 8
STAGE2_UNROLL = 8


def _cs(rows, cols, n):
    m = (np.outer(np.arange(rows), np.arange(cols)) % n).astype(np.float64)
    ang = 2.0 * np.pi * m / n
    return np.cos(ang), np.sin(ang)


def _tables():
    t = {}
    c, s = _cs(FOURIER_GROUP_DIM, FOURIER_GROUP_DIM, FOURIER_GROUP_DIM)
    t['chan'] = np.concatenate([c, -s], axis=1) / math.sqrt(SEQ * FOURIER_GROUP_DIM)
    c, s = _cs(N1F, N1F, N1F)
    t['m1f'] = np.block([[c, s], [-s, c]])
    m = (np.arange(N1C)[None, :, None] * (N2 * np.arange(CHUNKS)[None, None, :]
                                          + np.arange(N2)[:, None, None])) % (2 * SEQ)
    ang = 2.0 * np.pi * m.astype(np.float64) / (2 * SEQ)
    c, s = np.cos(ang), np.sin(ang)
    t['m1t'] = np.concatenate([np.concatenate([c, s], axis=2), np.concatenate([-s, c], axis=2)], axis=1)
    c, s = c.transpose(0, 2, 1), s.transpose(0, 2, 1)
    t['m1it'] = np.concatenate([np.concatenate([c, -s], axis=2), np.concatenate([s, c], axis=2)], axis=1)
    c, s = _cs(N1C, N1C, N1C)
    order = list(range(CHUNKS)) + list(range(N1C - 1, CHUNKS - 1, -1))
    t['m1k'] = np.concatenate([c[:, order], -s[:, order]], axis=0)
    c, s = _cs(N2, N2, N2)
    t['m2'] = np.block([[c, s], [-s, c]])
    t['m2i'] = np.block([[c, -s], [s, c]])
    t['m2r'] = np.concatenate([c, s], axis=1)
    return {k: jnp.asarray(v.astype(np.float32)) for k, v in t.items()}


def _twiddles(n1, n):
    r = np.arange(N2 * n1)
    ang = 2.0 * np.pi * ((r // n1) * (r % n1)).astype(np.float64) / n
    wide = lambda v: jnp.asarray(np.broadcast_to(v.astype(np.float32)[:, None], (N2 * n1, LANES)))
    return wide(np.cos(ang)), wide(np.sin(ang))


def _const_spec(shape):
    nd = len(shape)
    return pl.BlockSpec(shape, lambda *_: (0,) * nd, pipeline_mode=pl.Buffered(1))


def _params(sem):
    return pltpu.CompilerParams(dimension_semantics=sem, vmem_limit_bytes=VMEM_LIMIT)


def _prep_kernel(w_ref, g_ref, chan_ref, o_ref):
    g = g_ref[...]
    gd = FOURIER_GROUP_DIM
    for q in range(FOURIER_WIDTH // gd):
        z = jnp.dot(w_ref[:, q * gd:(q + 1) * gd], chan_ref[...], preferred_element_type=F32,
                    precision=lax.Precision.HIGHEST)
        o_ref[:, q * gd:(q + 1) * gd] = (g * z[:, :gd]).astype(BF16)
        o_ref[:, FOURIER_WIDTH + q * gd:FOURIER_WIDTH + (q + 1) * gd] = (g * z[:, gd:]).astype(BF16)
    o_ref[:, 2 * FOURIER_WIDTH:] = (g * w_ref[:, FOURIER_WIDTH:]).astype(BF16)


def _prep_weights(w_in, g_mix, chan):
    return pl.pallas_call(
        _prep_kernel,
        out_shape=jax.ShapeDtypeStruct((D_MODEL, IN_COLS), BF16),
        compiler_params=_params(None),
        name='prep_weights',
    )(w_in, g_mix.reshape(D_MODEL, 1), chan)


ROW_TILE = 512
TILE_CHUNKS = ROW_TILE // N2
ROW_TILES = SEQ // ROW_TILE
FCOLS = 2 * FOURIER_WIDTH
HCOLS = 3 * HYENA_WIDTH
SUBLANES = 8


def _rms_scale(x):
    return x * lax.rsqrt(jnp.mean(x * x, axis=-1, keepdims=True) + EPS)


def _edge_kernel(x_ref, w_ref, o_ref):
    o_ref[...] = jnp.dot(_rms_scale(x_ref[...]).astype(BF16), w_ref[:, FCOLS:],
                         preferred_element_type=F32)


def _edge_rows(x, w):
    xe = x.reshape(BATCH, ROW_TILES, ROW_TILE, D_MODEL)[:, :, (0, ROW_TILE - 1), :]
    out = pl.pallas_call(
        _edge_kernel,
        out_shape=jax.ShapeDtypeStruct((BATCH * ROW_TILES * 2, HCOLS), F32),
        compiler_params=_params(None),
        name='in_proj_edges',
    )(xe.reshape(BATCH * ROW_TILES * 2, D_MODEL), w)
    return out.reshape(BATCH, ROW_TILES, 2, HCOLS)


def _inproj_kernel(x_ref, w_ref, ep_ref, en_ref, cw_ref, cb_ref, o_ref):
    i = pl.program_id(1)
    h = _rms_scale(x_ref[0]).astype(BF16)
    ph = jnp.dot(h, w_ref[:, FCOLS:], preferred_element_type=F32)
    w0 = cw_ref[0:1, :]
    w1 = cw_ref[1:2, :]
    w2 = cw_ref[2:3, :]
    bias = cb_ref[...]
    up = pltpu.roll(ph, 1, axis=0)
    dn = pltpu.roll(ph, ROW_TILE - 1, axis=0)
    y = up * w0 + ph * w1 + dn * w2 + bias
    row = lax.broadcasted_iota(jnp.int32, (SUBLANES, HCOLS), 0)
    before = ep_ref[0, 0, 1:2, :] * (i > 0).astype(F32)
    after = en_ref[0, 0, 0:1, :] * (i < ROW_TILES - 1).astype(F32)
    last = ROW_TILE - SUBLANES
    y_first = (jnp.where(row == 0, before, up[:SUBLANES]) * w0 + ph[:SUBLANES] * w1
               + dn[:SUBLANES] * w2 + bias)
    y_last = (up[last:] * w0 + ph[last:] * w1
              + jnp.where(row == SUBLANES - 1, after, dn[last:]) * w2 + bias)
    pf = jnp.dot(h, w_ref[:, :FCOLS], preferred_element_type=F32)
    for j in range(TILE_CHUNKS):
        lo = j * N2
        o_ref[0, j * CHUNK_PITCH:j * CHUNK_PITCH + N2, :FCOLS] = pf[lo:lo + N2]
        o_ref[0, j * CHUNK_PITCH:j * CHUNK_PITCH + N2, FCOLS:] = y[lo:lo + N2]
        o_ref[0, j * CHUNK_PITCH + N2:(j + 1) * CHUNK_PITCH, :] = jnp.zeros(
            (CHUNK_PITCH - N2, IN_COLS), F32)
    o_ref[0, 0:SUBLANES, FCOLS:] = y_first
    end = (TILE_CHUNKS - 1) * CHUNK_PITCH + N2
    o_ref[0, end - SUBLANES:end, FCOLS:] = y_last


def _inproj(x, w, conv_w, conv_b):
    edges = _edge_rows(x, w)
    edge_blk = (1, 1, 2, HCOLS)
    return pl.pallas_call(
        _inproj_kernel,
        grid=(BATCH, ROW_TILES),
        in_specs=[pl.BlockSpec((1, ROW_TILE, D_MODEL), lambda b, i: (b, i, 0)),
                  _const_spec((D_MODEL, IN_COLS)),
                  pl.BlockSpec(edge_blk, lambda b, i: (b, jnp.maximum(i - 1, 0), 0, 0)),
                  pl.BlockSpec(edge_blk, lambda b, i: (b, jnp.minimum(i + 1, ROW_TILES - 1), 0, 0)),
                  _const_spec((3, HCOLS)), _const_spec((1, HCOLS))],
        out_specs=pl.BlockSpec((1, TILE_CHUNKS * CHUNK_PITCH, IN_COLS), lambda b, i: (b, i, 0)),
        out_shape=jax.ShapeDtypeStruct((BATCH, SEQ_ROWS, IN_COLS), F32),
        compiler_params=_params(('parallel', 'parallel')),
        name='in_proj',
    )(x, w, edges, edges, conv_w, conv_b.reshape(1, -1))


def _rows(ref, b, n2):
    return ref[b, pl.ds(n2, CHUNKS, stride=CHUNK_PITCH), :]


def _store_stage1(a_ref, a, n1, n2, twc_ref, tws_ref):
    for h in range(2):
        ar = a[:n1, h * LANES:(h + 1) * LANES]
        ai = a[n1:, h * LANES:(h + 1) * LANES]
        row0 = pl.multiple_of((n2 + h) * n1, n1)
        c = twc_ref[pl.ds(row0, n1), :]
        s = tws_ref[pl.ds(row0, n1), :]
        a_ref[pl.ds(n2 + h, n1, stride=SLAB), :] = ar * c + ai * s
        a_ref[pl.ds(N2 + n2 + h, n1, stride=SLAB), :] = ai * c - ar * s


def _load_slab_pair(a_ref, k1):
    off0 = pl.multiple_of(k1 * SLAB, 8)
    off1 = pl.multiple_of((k1 + 1) * SLAB, 8)
    x = jnp.concatenate([a_ref[pl.ds(off0, 2 * N2), :], a_ref[pl.ds(off1, 2 * N2), :]], axis=1)
    return x, off0, off1


def _fourier_kernel(zr_ref, zi_ref, m1_ref, m2_ref, twc_ref, tws_ref, o_ref, a_ref):
    def stage1(j, carry):
        n2 = 2 * j
        cols = []
        for h in range(2):
            cols.append(jnp.concatenate(
                [zr_ref[0, pl.ds(n2 + h, N1F, stride=CHUNK_PITCH), :],
                 zi_ref[0, pl.ds(n2 + h, N1F, stride=CHUNK_PITCH), :]], axis=0))
        x = jnp.concatenate(cols, axis=1).astype(BF16)
        a = jnp.dot(m1_ref[...], x, preferred_element_type=F32)
        _store_stage1(a_ref, a, N1F, n2, twc_ref, tws_ref)
        return carry

    lax.fori_loop(0, N2 // 2, stage1, 0, unroll=STAGE1_UNROLL)

    def stage2(j, carry):
        k1 = 2 * j
        x, _, _ = _load_slab_pair(a_ref, k1)
        y = jnp.dot(m2_ref[...], x.astype(BF16), preferred_element_type=F32)
        o_ref[0, pl.ds(k1, N2, stride=FPITCH), :] = y[:, :LANES]
        o_ref[0, pl.ds(k1 + 1, N2, stride=FPITCH), :] = y[:, LANES:]
        return carry

    lax.fori_loop(0, N1F // 2, stage2, 0, unroll=STAGE2_UNROLL)
    for i in range(FPITCH - N1F):
        o_ref[0, pl.ds(N1F + i, N2, stride=FPITCH), :] = jnp.zeros((N2, LANES), F32)


def _fourier(p, m1, m2r, twc, tws):
    nblk = FOURIER_WIDTH // LANES
    return pl.pallas_call(
        _fourier_kernel,
        grid=(BATCH, nblk),
        in_specs=[pl.BlockSpec((1, SEQ_ROWS, LANES), lambda b, c: (b, 0, c)),
                  pl.BlockSpec((1, SEQ_ROWS, LANES), lambda b, c: (b, 0, nblk + c)),
                  _const_spec(m1.shape), _const_spec(m2r.shape),
                  _const_spec(twc.shape), _const_spec(tws.shape)],
        out_specs=pl.BlockSpec((1, FROWS, LANES), lambda b, c: (b, 0, c)),
        out_shape=jax.ShapeDtypeStruct((BATCH, FROWS, FOURIER_WIDTH), F32),
        scratch_shapes=[pltpu.VMEM((N1F * SLAB, LANES), F32)],
        compiler_params=_params(('parallel', 'parallel')),
        name='fourier_fft',
    )(p, p, m1, m2r, twc, tws)


FILT_ROWS = 512


def _filter_kernel(feat_ref, w0_ref, b0_ref, wi_ref, bi_ref, fr_ref, wf_ref, wb_ref, dl_ref,
                   m1_ref, m2_ref, twc_ref, tws_ref, o_ref, h_ref, k_ref, a_ref):
    hi = lax.Precision.HIGHEST
    nchunk = SEQ // FILT_ROWS
    blocks = FILT_ROWS // CHUNKS

    @pl.when(pl.program_id(0) == 0)
    def _():
        fr = fr_ref[...]

        def mlp(i, carry):
            r0 = pl.multiple_of(i * FILT_ROWS, FILT_ROWS)
            h = jnp.sin(fr * (jnp.dot(feat_ref[pl.ds(r0, FILT_ROWS), :], w0_ref[...],
                                      preferred_element_type=F32, precision=hi) + b0_ref[...]))
            for j in range(FILTER_INNER_LAYERS):
                h = jnp.sin(fr * (jnp.dot(h, wi_ref[j], preferred_element_type=F32, precision=hi)
                                  + bi_ref[j]))
            h_ref[pl.ds(r0, FILT_ROWS), :] = h
            return carry

        lax.fori_loop(0, nchunk, mlp, 0)

    w_both = jnp.concatenate([wf_ref[...], wb_ref[...]], axis=1)
    dl = jnp.abs(dl_ref[...])

    def taps(i, carry):
        r0 = pl.multiple_of(i * FILT_ROWS, FILT_ROWS)
        r = r0 + lax.broadcasted_iota(jnp.int32, (FILT_ROWS, LANES), 0)
        pos = (r % CHUNKS) * N2 + r // CHUNKS
        decay = jnp.exp(-(pos.astype(F32) / F32(SEQ - 1)) * dl)
        k = jnp.dot(h_ref[pl.ds(r0, FILT_ROWS), :], w_both, preferred_element_type=F32, precision=hi)
        kf = k[:, :LANES] * decay
        kb = k[:, LANES:] * decay
        for q in range(blocks):
            n2 = i * blocks + q
            k_ref[pl.ds(pl.multiple_of(n2 * N1C, N1C), CHUNKS), :] = kf[q * CHUNKS:(q + 1) * CHUNKS]
            dst = ((N2 - n2) % N2) * N1C + CHUNKS
            k_ref[pl.ds(pl.multiple_of(dst, CHUNKS), CHUNKS), :] = kb[q * CHUNKS:(q + 1) * CHUNKS]
        return carry

    lax.fori_loop(0, nchunk, taps, 0)
    row = lax.broadcasted_iota(jnp.int32, (CHUNKS, LANES), 0)
    kb0 = k_ref[CHUNKS:2 * CHUNKS, :]
    k_ref[0:CHUNKS, :] = k_ref[0:CHUNKS, :] + jnp.where(row == 0, kb0, 0.0)
    k_ref[CHUNKS:2 * CHUNKS, :] = jnp.where(row == CHUNKS - 1, 0.0, pltpu.roll(kb0, CHUNKS - 1, axis=0))

    def sq(i, ss):
        k = k_ref[pl.ds(pl.multiple_of(i * FILT_ROWS, FILT_ROWS), FILT_ROWS), :]
        return ss + jnp.sum(k * k, axis=0, keepdims=True)

    sumsq = lax.fori_loop(0, 2 * nchunk, sq, jnp.zeros((1, LANES), F32))
    scale = lax.rsqrt(sumsq + EPS) * F32(1.0 / (2 * SEQ))

    def stage1(j, carry):
        n2 = 2 * j
        off0 = pl.multiple_of(n2 * N1C, N1C)
        off1 = pl.multiple_of((n2 + 1) * N1C, N1C)
        x = jnp.concatenate([k_ref[pl.ds(off0, N1C), :], k_ref[pl.ds(off1, N1C), :]], axis=1)
        a = jnp.dot(m1_ref[...], x.astype(BF16), preferred_element_type=F32)
        _store_stage1(a_ref, a, N1C, n2, twc_ref, tws_ref)
        return carry

    lax.fori_loop(0, N2 // 2, stage1, 0, unroll=STAGE1_UNROLL)

    def stage2(j, carry):
        k1 = 2 * j
        x, _, _ = _load_slab_pair(a_ref, k1)
        y = jnp.dot(m2_ref[...], x.astype(BF16), preferred_element_type=F32)
        o0 = pl.multiple_of(k1 * 2 * N2, 2 * N2)
        o_ref[0, pl.ds(o0, 2 * N2), :] = (y[:, :LANES] * scale).astype(BF16)
        o_ref[0, pl.ds(o0 + 2 * N2, 2 * N2), :] = (y[:, LANES:] * scale).astype(BF16)
        return carry

    lax.fori_loop(0, N1C // 2, stage2, 0, unroll=STAGE2_UNROLL)


def _filters(feat, w0, b0, wi, bi, fr, w_out, deltas, m1k, m2, twc, tws):
    nblk = HYENA_WIDTH // LANES
    rows = 2 * SEQ
    return pl.pallas_call(
        _filter_kernel,
        grid=(HYENA_ORDER * nblk,),
        in_specs=[_const_spec(feat.shape), _const_spec(w0.shape), _const_spec(b0.shape),
                  _const_spec(wi.shape), _const_spec(bi.shape), _const_spec(fr.shape),
                  pl.BlockSpec((FILTER_HIDDEN, LANES), lambda g: (0, (g // nblk) * 2 * nblk + g % nblk)),
                  pl.BlockSpec((FILTER_HIDDEN, LANES), lambda g: (0, (g // nblk) * 2 * nblk + nblk + g % nblk)),
                  pl.BlockSpec((1, LANES), lambda g: (0, g % nblk)),
                  _const_spec(m1k.shape), _const_spec(m2.shape),
                  _const_spec(twc.shape), _const_spec(tws.shape)],
        out_specs=pl.BlockSpec((1, N1C * 2 * N2, LANES), lambda g: (g // nblk, 0, g % nblk)),
        out_shape=jax.ShapeDtypeStruct((HYENA_ORDER, N1C * 2 * N2, HYENA_WIDTH), BF16),
        scratch_shapes=[pltpu.VMEM((SEQ, FILTER_HIDDEN), F32),
                        pltpu.VMEM((rows, LANES), F32),
                        pltpu.VMEM((N1C * SLAB, LANES), F32)],
        compiler_params=_params(('arbitrary',)),
        name='hyena_filters',
    )(feat, w0, b0, wi, bi, fr, w_out, w_out, deltas, m1k, m2, twc, tws)


def _conv_kernel(u_ref, g_ref, kf_ref, d_ref, m1_ref, m2_ref, m2i_ref, m1i_ref, o_ref, a_ref):
    def stage1(j, carry):
        for h in range(2):
            n2 = 2 * j + h
            x = jnp.concatenate([_rows(u_ref, 0, n2), _rows(u_ref, 1, n2)], axis=0).astype(BF16)
            a = jnp.dot(m1_ref[n2], x, preferred_element_type=F32)
            a_ref[pl.ds(n2, N1C, stride=SLAB), :] = a[:N1C]
            a_ref[pl.ds(N2 + n2, N1C, stride=SLAB), :] = a[N1C:]
        return carry

    lax.fori_loop(0, N2 // 2, stage1, 0, unroll=STAGE1_UNROLL)

    def stage2(j, carry):
        k1 = 2 * j
        x, off0, off1 = _load_slab_pair(a_ref, k1)
        s = jnp.dot(m2_ref[...], x.astype(BF16), preferred_element_type=F32)
        ko = pl.multiple_of(k1 * 2 * N2, 2 * N2)
        kf = kf_ref[0, pl.ds(ko, 4 * N2), :].astype(F32)
        kr = jnp.concatenate([kf[0:N2], kf[2 * N2:3 * N2]], axis=1)
        ki = jnp.concatenate([kf[N2:2 * N2], kf[3 * N2:4 * N2]], axis=1)
        xr = s[:N2]
        xi = s[N2:]
        y = jnp.concatenate([xr * kr - xi * ki, xr * ki + xi * kr], axis=0).astype(BF16)
        b = jnp.dot(m2i_ref[...], y, preferred_element_type=F32)
        a_ref[pl.ds(off0, 2 * N2), :] = b[:, :LANES]
        a_ref[pl.ds(off1, 2 * N2), :] = b[:, LANES:]
        return carry

    lax.fori_loop(0, N1C // 2, stage2, 0, unroll=STAGE2_UNROLL)

    d = d_ref[0]

    def stage3(j, carry):
        for h in range(2):
            t2 = 2 * j + h
            x = jnp.concatenate([a_ref[pl.ds(t2, N1C, stride=SLAB), :],
                                 a_ref[pl.ds(N2 + t2, N1C, stride=SLAB), :]], axis=0).astype(BF16)
            y = jnp.dot(m1i_ref[t2], x, preferred_element_type=F32)
            for b in range(2):
                yy = y[b * CHUNKS:(b + 1) * CHUNKS]
                o_ref[b, pl.ds(t2, CHUNKS, stride=CHUNK_PITCH), :] = (
                    _rows(g_ref, b, t2) * (yy + d * _rows(u_ref, b, t2)))
        return carry

    lax.fori_loop(0, N2 // 2, stage3, 0, unroll=STAGE1_UNROLL)
    for b in range(2):
        for i in range(CHUNK_PITCH - N2):
            o_ref[b, pl.ds(N2 + i, CHUNKS, stride=CHUNK_PITCH), :] = jnp.zeros((CHUNKS, LANES), F32)


def _longconv(u_arr, u_blk, g_arr, g_blk, kf, order, long_d, m1t, m2, m2i, m1it):
    nblk = HYENA_WIDTH // LANES
    blk = (2, SEQ_ROWS, LANES)
    consts = (m1t, m2, m2i, m1it)
    return pl.pallas_call(
        _conv_kernel,
        grid=(nblk, BATCH // 2),
        in_specs=[pl.BlockSpec(blk, lambda c, b: (b, 0, u_blk + c)),
                  pl.BlockSpec(blk, lambda c, b: (b, 0, g_blk + c)),
                  pl.BlockSpec((1, N1C * 2 * N2, LANES), lambda c, b: (order, 0, c)),
                  pl.BlockSpec((1, 1, LANES), lambda c, b: (order, 0, c))]
        + [_const_spec(c.shape) for c in consts],
        out_specs=pl.BlockSpec(blk, lambda c, b: (b, 0, c)),
        out_shape=jax.ShapeDtypeStruct((BATCH, SEQ_ROWS, HYENA_WIDTH), F32),
        scratch_shapes=[pltpu.VMEM((N1C * SLAB, LANES), F32)],
        compiler_params=_params(('parallel', 'parallel')),
        name=f'hyena_conv{order}',
    )(u_arr, g_arr, kf, long_d.reshape(HYENA_ORDER, 1, HYENA_WIDTH), *consts)


FF_CHUNK = 1024
TAIL_TILE = 1024


def _tail_kernel(x_ref, yf_ref, yh_ref, gf_ref, gh_ref, wo_ref, gm_ref, w1_ref, w2_ref, gl_ref, o_ref):
    def norm(v, g):
        return v * lax.rsqrt(jnp.mean(v * v, axis=-1, keepdims=True) + EPS) * g

    yf = jnp.concatenate([yf_ref[0, j * FPITCH:j * FPITCH + N1F, :]
                          for j in range(TAIL_TILE // N1F)], axis=0)
    yh = jnp.concatenate([yh_ref[0, j * CHUNK_PITCH:j * CHUNK_PITCH + N2, :]
                          for j in range(TAIL_TILE // N2)], axis=0)
    y = jnp.concatenate([norm(yf, gf_ref[...]), norm(yh, gh_ref[...])], axis=1).astype(BF16)
    xm = x_ref[0] + jnp.dot(y, wo_ref[...], preferred_element_type=F32)
    hm = norm(xm, gm_ref[...]).astype(BF16)
    acc = xm
    for c in range(D_FF // FF_CHUNK):
        a = jnp.dot(hm, w1_ref[:, c * FF_CHUNK:(c + 1) * FF_CHUNK], preferred_element_type=F32)
        a = jnp.square(jnp.maximum(a, 0.0)).astype(BF16)
        acc = acc + jnp.dot(a, w2_ref[c * FF_CHUNK:(c + 1) * FF_CHUNK, :], preferred_element_type=F32)
    o_ref[0] = norm(acc, gl_ref[...])


def _tail(x, yf, yh, g_f, g_h, wo, g_mlp, w1, w2, g_final):
    row = lambda v: v.reshape(1, -1)
    return pl.pallas_call(
        _tail_kernel,
        grid=(BATCH, SEQ // TAIL_TILE),
        in_specs=[pl.BlockSpec((1, TAIL_TILE, D_MODEL), lambda b, i: (b, i, 0)),
                  pl.BlockSpec((1, TAIL_TILE // N1F * FPITCH, FOURIER_WIDTH), lambda b, i: (b, i, 0)),
                  pl.BlockSpec((1, TAIL_TILE // N2 * CHUNK_PITCH, HYENA_WIDTH), lambda b, i: (b, i, 0)),
                  _const_spec((1, FOURIER_WIDTH)), _const_spec((1, HYENA_WIDTH)),
                  _const_spec((D_MODEL, D_MODEL)), _const_spec((1, D_MODEL)),
                  _const_spec((D_MODEL, D_FF)), _const_spec((D_FF, D_MODEL)),
                  _const_spec((1, D_MODEL))],
        out_specs=pl.BlockSpec((1, TAIL_TILE, D_MODEL), lambda b, i: (b, i, 0)),
        out_shape=jax.ShapeDtypeStruct((BATCH, SEQ, D_MODEL), F32),
        compiler_params=_params(('parallel', 'parallel')),
        name='out_proj_mlp',
    )(x, yf, yh, row(g_f), row(g_h), wo, row(g_mlp), w1, w2, row(g_final))


def _filter_features():
    r = np.arange(SEQ)
    pos = ((r % CHUNKS) * N2 + r // CHUNKS).astype(np.float64)
    t = pos / (SEQ - 1)
    bands = np.linspace(1e-4, POS_BANDS - 1, POS_BANDS)
    ang = (2.0 * np.pi * pos / SEQ)[:, None] * bands[None, :]
    z = np.concatenate([t[:, None], np.cos(ang), -np.sin(ang)], axis=-1)
    return jnp.asarray(np.pad(z, ((0, 0), (0, LANES - POS_EMB_DIM))).astype(np.float32))


def kernel(x, g_mix, w_in, conv_w, conv_b, filt_w0, filt_b0, filt_w_inner, filt_b_inner, filt_freq,
           filt_w_out, long_d, g_fourier, g_hyena, w_out, g_mlp, w_fc1, w_fc2, g_final):
    tb = _tables()
    bf = lambda v: v.astype(BF16)
    twc_f, tws_f = _twiddles(N1F, SEQ)
    twc_c, tws_c = _twiddles(N1C, 2 * SEQ)

    w0 = jnp.pad(filt_w0[0], ((0, LANES - POS_EMB_DIM), (0, 0)))
    deltas = jnp.linspace(MIN_DECAY, MAX_DECAY, HYENA_WIDTH, dtype=F32).reshape(1, HYENA_WIDTH)
    kf = _filters(_filter_features(), w0, filt_b0[0].reshape(1, -1), filt_w_inner[0],
                  filt_b_inner[0].reshape(FILTER_INNER_LAYERS, 1, FILTER_HIDDEN),
                  filt_freq[0].reshape(1, -1), filt_w_out[0], deltas,
                  bf(tb['m1k']), bf(tb['m2']), twc_c, tws_c)

    w_prep = _prep_weights(w_in[0], g_mix[0], tb['chan'])
    p = _inproj(x, w_prep, conv_w[0], conv_b[0])
    yf = _fourier(p, bf(tb['m1f']), bf(tb['m2r']), twc_f, tws_f)
    nblk = HYENA_WIDTH // LANES
    x1_blk = 2 * FOURIER_WIDTH // LANES
    fft = (bf(tb['m1t']), bf(tb['m2']), bf(tb['m2i']), bf(tb['m1it']))
    z1 = _longconv(p, x1_blk + 2 * nblk, p, x1_blk, kf, 0, long_d[0], *fft)
    z2 = _longconv(z1, 0, p, x1_blk + nblk, kf, 1, long_d[0], *fft)

    return _tail(x, yf, z2, g_fourier[0], g_hyena[0], bf(w_out[0]), g_mlp[0],
                 bf(w_fc1[0]), bf(w_fc2[0]), g_final)
```

```python
import functools
import math

import numpy as np
import jax
import jax.numpy as jnp
from jax import lax
from jax.experimental import pallas as pl
from jax.experimental.pallas import tpu as pltpu

F32 = jnp.float32
BF16 = jnp.bfloat16

D_MODEL = 1024
BATCH = 8
SEQ = 4096
FOURIER_WIDTH = 512
FOURIER_GROUP_DIM = 64
HYENA_WIDTH = 512
HYENA_ORDER = 2
POS_BANDS = 16
POS_EMB_DIM = 1 + 2 * POS_BANDS
FILTER_HIDDEN = 64
FILTER_INNER_LAYERS = 2
D_FF = 4 * D_MODEL
DECAY_TARGET = 1e-2
MAX_DECAY = math.log(DECAY_TARGET) / 0.3
MIN_DECAY = math.log(DECAY_TARGET) / 1.5
EPS = 1e-5

LANES = 128
N2 = 128
CHUNKS = SEQ // N2
CHUNK_PITCH = N2 + 8
SEQ_ROWS = CHUNKS * CHUNK_PITCH
N1F = SEQ // N2
N1C = 2 * SEQ // N2
ROWSET = 2 * N1C + 8
FROWSET = 2 * N1F + 8
FPITCH = N1F + 8
FROWS = N2 * FPITCH
IN_COLS = FOURIER_WIDTH + 3 * HYENA_WIDTH
VMEM_LIMIT = 60 * 1024 * 1024
STAGE2_UNROLL = 32
CONV_UNROLL = 64


def _cs(rows, cols, n):
    m = (np.outer(np.arange(rows), np.arange(cols)) % n).astype(np.float64)
    ang = 2.0 * np.pi * m / n
    return np.cos(ang), np.sin(ang)


def _tables():
    t = {}
    c, s = _cs(FOURIER_GROUP_DIM, FOURIER_GROUP_DIM, FOURIER_GROUP_DIM)
    eye = np.eye(LANES // FOURIER_GROUP_DIM)
    t['chan'] = (np.concatenate([np.kron(eye, c), -np.kron(eye, s)], axis=1)
                 / math.sqrt(SEQ * FOURIER_GROUP_DIM))
    m = (np.arange(N1F)[None, :, None] * (N2 * np.arange(N1F)[None, None, :]
                                          + np.arange(N2)[:, None, None])) % SEQ
    ang = 2.0 * np.pi * m.astype(np.float64) / SEQ
    c, s = np.cos(ang), np.sin(ang)
    t['m1f'] = np.concatenate([np.concatenate([c, s], axis=2), np.concatenate([-s, c], axis=2)], axis=1)
    m = (np.arange(N1C)[None, :, None] * (N2 * np.arange(CHUNKS)[None, None, :]
                                          + np.arange(N2)[:, None, None])) % (2 * SEQ)
    ang = 2.0 * np.pi * m.astype(np.float64) / (2 * SEQ)
    c, s = np.cos(ang), np.sin(ang)
    t['m1t'] = np.concatenate([np.concatenate([c, s], axis=2), np.concatenate([-s, c], axis=2)], axis=1)
    c, s = c.transpose(0, 2, 1), s.transpose(0, 2, 1)
    t['m1it'] = np.concatenate([np.concatenate([c, -s], axis=2), np.concatenate([s, c], axis=2)], axis=1)
    order = np.array(list(range(CHUNKS)) + list(range(N1C - 1, CHUNKS - 1, -1)))
    m = (np.arange(N1C)[None, :, None] * (N2 * order[None, None, :]
                                          + np.arange(N2)[:, None, None])) % (2 * SEQ)
    ang = 2.0 * np.pi * m.astype(np.float64) / (2 * SEQ)
    t['m1k'] = np.concatenate([np.cos(ang), -np.sin(ang)], axis=1)
    c, s = _cs(N2, N2, N2)
    t['m2'] = np.block([[c, s], [-s, c]])
    t['m2i'] = np.block([[c, -s], [s, c]])
    t['m2r'] = np.concatenate([c, s], axis=1)
    return {k: jnp.asarray(v.astype(np.float32)) for k, v in t.items()}


def _const_spec(shape):
    nd = len(shape)
    return pl.BlockSpec(shape, lambda *_: (0,) * nd, pipeline_mode=pl.Buffered(1))


def _params(sem):
    return pltpu.CompilerParams(dimension_semantics=sem, vmem_limit_bytes=VMEM_LIMIT)


ROW_TILE = 1024
TILE_CHUNKS = ROW_TILE // N2
ROW_TILES = SEQ // ROW_TILE
FCOLS = FOURIER_WIDTH
HCOLS = 3 * HYENA_WIDTH
SUBLANES = 8


def _rmsnorm(x, g):
    return x * lax.rsqrt(jnp.mean(x * x, axis=-1, keepdims=True) + EPS) * g


def _inproj_kernel(x_ref, g_ref, w_ref, of_ref, o_ref):
    h = _rmsnorm(x_ref[0], g_ref[...]).astype(BF16)
    ph = jnp.dot(h, w_ref[:, FCOLS:], preferred_element_type=F32)
    for j in range(TILE_CHUNKS):
        o_ref[0, j * CHUNK_PITCH:j * CHUNK_PITCH + N2, :] = ph[j * N2:(j + 1) * N2]
        o_ref[0, j * CHUNK_PITCH + N2:(j + 1) * CHUNK_PITCH, :] = jnp.zeros(
            (CHUNK_PITCH - N2, HCOLS), F32)
    of_ref[0] = jnp.dot(h, w_ref[:, :FCOLS], preferred_element_type=F32).astype(BF16)


def _inproj(x, g, w):
    return pl.pallas_call(
        _inproj_kernel,
        grid=(BATCH, ROW_TILES),
        in_specs=[pl.BlockSpec((1, ROW_TILE, D_MODEL), lambda b, i: (b, i, 0)),
                  _const_spec((1, D_MODEL)), _const_spec((D_MODEL, IN_COLS))],
        out_specs=[pl.BlockSpec((1, ROW_TILE, FCOLS), lambda b, i: (b, i, 0)),
                   pl.BlockSpec((1, TILE_CHUNKS * CHUNK_PITCH, HCOLS), lambda b, i: (b, i, 0))],
        out_shape=[jax.ShapeDtypeStruct((BATCH, SEQ, FCOLS), BF16),
                   jax.ShapeDtypeStruct((BATCH, SEQ_ROWS, HCOLS), F32)],
        compiler_params=_params(('parallel', 'parallel')),
        name='in_proj',
    )(x, g.reshape(1, D_MODEL), w)


def _rows(ref, b, n2):
    return ref[b, pl.ds(n2, CHUNKS, stride=CHUNK_PITCH), :]


def _pair_loop(body):
    lax.fori_loop(0, N2 // 2, lambda j, c: (body(j), c)[1], 0, unroll=CONV_UNROLL)
    body(0)
    body(N2 // 2 - 1)


def _shortconv_pair(ref, b, j, w_ref, b_ref):
    n2 = 2 * j
    row = lax.broadcasted_iota(jnp.int32, (CHUNKS, LANES), 0)
    c0 = _rows(ref, b, n2)
    c1 = _rows(ref, b, n2 + 1)
    if isinstance(j, int) and j == 0:
        before = jnp.where(row == 0, 0.0, pltpu.roll(_rows(ref, b, N2 - 1), 1, axis=0))
    elif isinstance(j, int):
        before = _rows(ref, b, n2 - 1)
    else:
        before = _rows(ref, b, jnp.maximum(n2 - 1, 0))
    if isinstance(j, int) and j == N2 // 2 - 1:
        after = jnp.where(row == CHUNKS - 1, 0.0, pltpu.roll(_rows(ref, b, 0), CHUNKS - 1, axis=0))
    else:
        after = _rows(ref, b, n2 + 2)
    w0 = w_ref[0:1, :]
    w1 = w_ref[1:2, :]
    w2 = w_ref[2:3, :]
    bias = b_ref[...]
    return (before * w0 + c0 * w1 + c1 * w2 + bias,
            c0 * w0 + c1 * w1 + after * w2 + bias)


CHAN_CHUNKS = 4


def _fourier_kernel(u_ref, chan_ref, m1_ref, m2_ref, o_ref, z_ref, a_ref):
    def chan(i, carry):
        u = u_ref[0, pl.ds(pl.multiple_of(i * CHAN_CHUNKS * N2, CHAN_CHUNKS * N2), CHAN_CHUNKS * N2), :]
        z = jnp.dot(u, chan_ref[...], preferred_element_type=F32)
        for q in range(CHAN_CHUNKS):
            r0 = pl.multiple_of((CHAN_CHUNKS * i + q) * CHUNK_PITCH, SUBLANES)
            z_ref[0, pl.ds(r0, N2), :] = z[q * N2:(q + 1) * N2, :LANES]
            z_ref[1, pl.ds(r0, N2), :] = z[q * N2:(q + 1) * N2, LANES:]
        return carry

    lax.fori_loop(0, CHUNKS // CHAN_CHUNKS, chan, 0, unroll=True)

    def stage1(j, carry):
        for h in range(2):
            n2 = 2 * j + h
            x = jnp.concatenate([_rows(z_ref, 0, n2), _rows(z_ref, 1, n2)], axis=0).astype(BF16)
            a = jnp.dot(m1_ref[n2], x, preferred_element_type=F32)
            a_ref[pl.ds(pl.multiple_of(n2 * FROWSET, SUBLANES), 2 * N1F), :] = a
        return carry

    lax.fori_loop(0, N2 // 2, stage1, 0, unroll=CONV_UNROLL)

    def stage2(j, carry):
        k1 = 2 * j
        col = lambda r: a_ref[pl.ds(r, N2, stride=FROWSET), :]
        x = jnp.concatenate([jnp.concatenate([col(k1 + h), col(N1F + k1 + h)], axis=0)
                             for h in range(2)], axis=1)
        y = jnp.dot(m2_ref[...], x.astype(BF16), preferred_element_type=F32)
        o_ref[0, pl.ds(k1, N2, stride=FPITCH), :] = y[:, :LANES]
        o_ref[0, pl.ds(k1 + 1, N2, stride=FPITCH), :] = y[:, LANES:]
        return carry

    lax.fori_loop(0, N1F // 2, stage2, 0, unroll=STAGE2_UNROLL)
    for i in range(FPITCH - N1F):
        o_ref[0, pl.ds(N1F + i, N2, stride=FPITCH), :] = jnp.zeros((N2, LANES), F32)


def _fourier(p, chan, m1, m2r):
    nblk = FOURIER_WIDTH // LANES
    consts = (chan, m1, m2r)
    return pl.pallas_call(
        _fourier_kernel,
        grid=(BATCH, nblk),
        in_specs=[pl.BlockSpec((1, SEQ, LANES), lambda b, c: (b, 0, c))]
        + [_const_spec(c.shape) for c in consts],
        out_specs=pl.BlockSpec((1, FROWS, LANES), lambda b, c: (b, 0, c)),
        out_shape=jax.ShapeDtypeStruct((BATCH, FROWS, FOURIER_WIDTH), F32),
        scratch_shapes=[pltpu.VMEM((2, SEQ_ROWS, LANES), F32),
                        pltpu.VMEM((N2 * FROWSET, LANES), F32)],
        compiler_params=_params(('parallel', 'parallel')),
        name='fourier_fft',
    )(p, *consts)


FILT_ROWS = 512


def _split(a):
    hi = a.astype(BF16)
    return hi, (a - hi.astype(F32)).astype(BF16)


def _dot3(a, b_hi, b_lo):
    a_hi, a_lo = _split(a)
    dot = lambda x, y: jnp.dot(x, y, preferred_element_type=F32)
    return dot(a_hi, b_hi) + (dot(a_lo, b_hi) + dot(a_hi, b_lo))


def _filter_kernel(feat_ref, w0_ref, b0_ref, wi_ref, bi_ref, fr_ref, wf_ref, wb_ref, dl_ref, sk_ref,
                   m1_ref, m2_ref, o_ref, h_ref, k_ref, a_ref):
    half = SEQ // 2
    nchunk = half // FILT_ROWS
    blocks = FILT_ROWS // CHUNKS

    @pl.when(pl.program_id(0) == 0)
    def _():
        fr = fr_ref[...]
        w0 = _split(w0_ref[...])
        wi = [_split(wi_ref[j]) for j in range(FILTER_INNER_LAYERS)]

        def mlp(i, carry):
            r0 = pl.multiple_of(i * FILT_ROWS, FILT_ROWS)
            h = jnp.sin(fr * (_dot3(feat_ref[pl.ds(r0, FILT_ROWS), :], *w0) + b0_ref[...]))
            for j in range(FILTER_INNER_LAYERS):
                h = jnp.sin(fr * (_dot3(h, *wi[j]) + bi_ref[j]))
            h_ref[pl.ds(r0, FILT_ROWS), :] = h
            return carry

        lax.fori_loop(0, nchunk, mlp, 0)

    w_both = jnp.concatenate([wf_ref[...], wb_ref[...]], axis=1)
    zero = jnp.zeros_like(w_both)
    w_pair = jnp.concatenate([jnp.concatenate([w_both, zero], axis=1),
                              jnp.concatenate([zero, w_both], axis=1)], axis=0)
    w_pair = _split(w_pair)
    dl = jnp.abs(dl_ref[...])

    def taps(i, carry):
        r0 = pl.multiple_of(i * FILT_ROWS, FILT_ROWS)
        k = _dot3(h_ref[pl.ds(r0, FILT_ROWS), :], *w_pair)
        for s in range(2):
            r = s * half + r0 + lax.broadcasted_iota(jnp.int32, (FILT_ROWS, LANES), 0)
            pos = (r % CHUNKS) * N2 + r // CHUNKS
            decay = jnp.exp(-(pos.astype(F32) / F32(SEQ - 1)) * dl)
            kf = k[:, 2 * s * LANES:(2 * s + 1) * LANES] * decay
            kb = k[:, (2 * s + 1) * LANES:(2 * s + 2) * LANES] * decay
            for q in range(blocks):
                n2 = s * (half // CHUNKS) + i * blocks + q
                k_ref[pl.ds(pl.multiple_of(n2 * N1C, N1C), CHUNKS), :] = kf[q * CHUNKS:(q + 1) * CHUNKS]
                dst = ((N2 - n2) % N2) * N1C + CHUNKS
                k_ref[pl.ds(pl.multiple_of(dst, CHUNKS), CHUNKS), :] = kb[q * CHUNKS:(q + 1) * CHUNKS]
        return carry

    lax.fori_loop(0, nchunk, taps, 0, unroll=True)
    row = lax.broadcasted_iota(jnp.int32, (CHUNKS, LANES), 0)
    kb0 = k_ref[CHUNKS:2 * CHUNKS, :]
    k_ref[0:CHUNKS, :] = k_ref[0:CHUNKS, :] + jnp.where(row == 0, kb0, 0.0)
    k_ref[CHUNKS:2 * CHUNKS, :] = jnp.where(row == CHUNKS - 1, 0.0, pltpu.roll(kb0, CHUNKS - 1, axis=0))

    def stage1(j, ss):
        for h in range(2):
            n2 = 2 * j + h
            x = k_ref[pl.ds(pl.multiple_of(n2 * N1C, N1C), N1C), :]
            a = jnp.dot(m1_ref[n2], x.astype(BF16), preferred_element_type=F32)
            a_ref[pl.ds(pl.multiple_of(n2 * ROWSET, SUBLANES), 2 * N1C), :] = a
            ss = ss + jnp.sum(x * x, axis=0, keepdims=True)
        return ss

    ss = lax.fori_loop(0, N2 // 2, stage1, jnp.zeros((1, LANES), F32), unroll=CONV_UNROLL)
    scale = lax.rsqrt(ss + EPS) * F32(1.0 / (2 * SEQ))
    skip = sk_ref[0] * F32(1.0 / (2 * SEQ))

    def stage2(j, carry):
        k1 = 2 * j
        col = lambda r: a_ref[pl.ds(r, N2, stride=ROWSET), :]
        x = jnp.concatenate([jnp.concatenate([col(k1 + h), col(N1C + k1 + h)], axis=0)
                             for h in range(2)], axis=1)
        y = jnp.dot(m2_ref[...], x.astype(BF16), preferred_element_type=F32)
        o0 = pl.multiple_of(k1 * 2 * N2, 2 * N2)
        for h in range(2):
            yh = y[:, h * LANES:(h + 1) * LANES] * scale
            o_ref[0, pl.ds(o0 + h * 2 * N2, N2), :] = (yh[:N2] + skip).astype(BF16)
            o_ref[0, pl.ds(o0 + h * 2 * N2 + N2, N2), :] = yh[N2:].astype(BF16)
        return carry

    lax.fori_loop(0, N1C // 2, stage2, 0, unroll=STAGE2_UNROLL)


def _filters(feat, w0, b0, wi, bi, fr, w_out, deltas, long_d, m1k, m2):
    nblk = HYENA_WIDTH // LANES
    rows = 2 * SEQ
    return pl.pallas_call(
        _filter_kernel,
        grid=(HYENA_ORDER * nblk,),
        in_specs=[_const_spec(feat.shape), _const_spec(w0.shape), _const_spec(b0.shape),
                  _const_spec(wi.shape), _const_spec(bi.shape), _const_spec(fr.shape),
                  pl.BlockSpec((FILTER_HIDDEN, LANES), lambda g: (0, (g // nblk) * 2 * nblk + g % nblk)),
                  pl.BlockSpec((FILTER_HIDDEN, LANES), lambda g: (0, (g // nblk) * 2 * nblk + nblk + g % nblk)),
                  pl.BlockSpec((1, LANES), lambda g: (0, g % nblk)),
                  pl.BlockSpec((1, 1, LANES), lambda g: (g // nblk, 0, g % nblk)),
                  _const_spec(m1k.shape), _const_spec(m2.shape)],
        out_specs=pl.BlockSpec((1, N1C * 2 * N2, LANES), lambda g: (g // nblk, 0, g % nblk)),
        out_shape=jax.ShapeDtypeStruct((HYENA_ORDER, N1C * 2 * N2, HYENA_WIDTH), BF16),
        scratch_shapes=[pltpu.VMEM((SEQ // 2, 2 * FILTER_HIDDEN), F32),
                        pltpu.VMEM((rows, LANES), F32),
                        pltpu.VMEM((N2 * ROWSET, LANES), F32)],
        compiler_params=_params(('arbitrary',)),
        name='hyena_filters',
    )(feat, w0, b0, wi, bi, fr, w_out, w_out, deltas,
      long_d.reshape(HYENA_ORDER, 1, HYENA_WIDTH), m1k, m2)


def _aligned(v, m):
    return v if isinstance(v, int) else pl.multiple_of(v, m)


def _conv_kernel(*refs, conv_u):
    if conv_u:
        (u_ref, g_ref, kf_ref, uw_ref, ub_ref, gw_ref, gb_ref, m1_ref, m2_ref, m2i_ref, m1i_ref,
         o_ref, a_ref) = refs
    else:
        u_ref, g_ref, kf_ref, gw_ref, gb_ref, m1_ref, m2_ref, m2i_ref, m1i_ref, o_ref, a_ref = refs

    def stage1(j):
        if conv_u:
            u = [_shortconv_pair(u_ref, b, j, uw_ref, ub_ref) for b in range(2)]
        else:
            u = [(_rows(u_ref, b, 2 * j), _rows(u_ref, b, 2 * j + 1)) for b in range(2)]
        for h in range(2):
            n2 = 2 * j + h
            x = jnp.concatenate([u[0][h], u[1][h]], axis=0).astype(BF16)
            a = jnp.dot(m1_ref[n2], x, preferred_element_type=F32)
            a_ref[pl.ds(_aligned(n2 * ROWSET, SUBLANES), 2 * N1C), :] = a

    _pair_loop(stage1)

    def stage2(j, carry):
        k1 = 2 * j
        col = lambda r: a_ref[pl.ds(r, N2, stride=ROWSET), :]
        x = jnp.concatenate([jnp.concatenate([col(k1 + h), col(N1C + k1 + h)], axis=0)
                             for h in range(2)], axis=1)
        s = jnp.dot(m2_ref[...], x.astype(BF16), preferred_element_type=F32)
        ko = pl.multiple_of(k1 * 2 * N2, 2 * N2)
        kf = kf_ref[0, pl.ds(ko, 4 * N2), :].astype(F32)
        kr = jnp.concatenate([kf[0:N2], kf[2 * N2:3 * N2]], axis=1)
        ki = jnp.concatenate([kf[N2:2 * N2], kf[3 * N2:4 * N2]], axis=1)
        xr = s[:N2]
        xi = s[N2:]
        y = jnp.concatenate([xr * kr - xi * ki, xr * ki + xi * kr], axis=0).astype(BF16)
        b = jnp.dot(m2i_ref[...], y, preferred_element_type=F32)
        for h in range(2):
            a_ref[pl.ds(k1 + h, N2, stride=ROWSET), :] = b[:N2, h * LANES:(h + 1) * LANES]
            a_ref[pl.ds(N1C + k1 + h, N2, stride=ROWSET), :] = b[N2:, h * LANES:(h + 1) * LANES]
        return carry

    lax.fori_loop(0, N1C // 2, stage2, 0, unroll=STAGE2_UNROLL)

    def stage3(j):
        g = [_shortconv_pair(g_ref, b, j, gw_ref, gb_ref) for b in range(2)]
        for h in range(2):
            t2 = 2 * j + h
            x = a_ref[pl.ds(_aligned(t2 * ROWSET, SUBLANES), 2 * N1C), :].astype(BF16)
            y = jnp.dot(m1i_ref[t2], x, preferred_element_type=F32)
            for b in range(2):
                o_ref[b, pl.ds(t2, CHUNKS, stride=CHUNK_PITCH), :] = (
                    g[b][h] * y[b * CHUNKS:(b + 1) * CHUNKS])

    _pair_loop(stage3)
    for b in range(2):
        for i in range(CHUNK_PITCH - N2):
            o_ref[b, pl.ds(N2 + i, CHUNKS, stride=CHUNK_PITCH), :] = jnp.zeros((CHUNKS, LANES), F32)


def _longconv(u_arr, u_blk, conv_u, g_arr, g_blk, kf, order, conv_w, conv_b, m1t, m2, m2i, m1it):
    nblk = HYENA_WIDTH // LANES
    blk = (2, SEQ_ROWS, LANES)
    args = [u_arr, g_arr, kf]
    specs = [pl.BlockSpec(blk, lambda c, b: (b, 0, u_blk + c)),
             pl.BlockSpec(blk, lambda c, b: (b, 0, g_blk + c)),
             pl.BlockSpec((1, N1C * 2 * N2, LANES), lambda c, b: (order, 0, c))]
    for blk0 in ([u_blk] if conv_u else []) + [g_blk]:
        args += [conv_w, conv_b.reshape(1, -1)]
        specs += [pl.BlockSpec((3, LANES), lambda c, b, o=blk0: (0, o + c)),
                  pl.BlockSpec((1, LANES), lambda c, b, o=blk0: (0, o + c))]
    consts = (m1t, m2, m2i, m1it)
    return pl.pallas_call(
        functools.partial(_conv_kernel, conv_u=conv_u),
        grid=(nblk, BATCH // 2),
        in_specs=specs + [_const_spec(c.shape) for c in consts],
        out_specs=pl.BlockSpec(blk, lambda c, b: (b, 0, c)),
        out_shape=jax.ShapeDtypeStruct((BATCH, SEQ_ROWS, HYENA_WIDTH), F32),
        scratch_shapes=[pltpu.VMEM((N2 * ROWSET, LANES), F32)],
        compiler_params=_params(('parallel', 'parallel')),
        name=f'hyena_conv{order}',
    )(*args, *consts)


FF_CHUNK = 1024
TAIL_TILE = 1024


def _tail_kernel(x_ref, yf_ref, yh_ref, gf_ref, gh_ref, wo_ref, gm_ref, w1_ref, w2_ref, gl_ref, o_ref,
                 a_ref):
    def norm(v, g):
        return v * lax.rsqrt(jnp.mean(v * v, axis=-1, keepdims=True) + EPS) * g

    yf = jnp.concatenate([yf_ref[0, j * FPITCH:j * FPITCH + N1F, :]
                          for j in range(TAIL_TILE // N1F)], axis=0)
    yh = jnp.concatenate([yh_ref[0, j * CHUNK_PITCH:j * CHUNK_PITCH + N2, :]
                          for j in range(TAIL_TILE // N2)], axis=0)
    y = jnp.concatenate([norm(yf, gf_ref[...]), norm(yh, gh_ref[...])], axis=1).astype(BF16)
    xm = x_ref[0] + jnp.dot(y, wo_ref[...], preferred_element_type=F32)
    hm = norm(xm, gm_ref[...]).astype(BF16)
    for c in range(D_FF // FF_CHUNK):
        a = jnp.dot(hm, w1_ref[:, c * FF_CHUNK:(c + 1) * FF_CHUNK], preferred_element_type=F32)
        a_ref[:, c * FF_CHUNK:(c + 1) * FF_CHUNK] = jnp.square(jnp.maximum(a, 0.0)).astype(BF16)
    acc = xm + jnp.dot(a_ref[...], w2_ref[...], preferred_element_type=F32)
    o_ref[0] = norm(acc, gl_ref[...])


def _tail(x, yf, yh, g_f, g_h, wo, g_mlp, w1, w2, g_final):
    row = lambda v: v.reshape(1, -1)
    return pl.pallas_call(
        _tail_kernel,
        grid=(BATCH, SEQ // TAIL_TILE),
        in_specs=[pl.BlockSpec((1, TAIL_TILE, D_MODEL), lambda b, i: (b, i, 0)),
                  pl.BlockSpec((1, TAIL_TILE // N1F * FPITCH, FOURIER_WIDTH), lambda b, i: (b, i, 0)),
                  pl.BlockSpec((1, TAIL_TILE // N2 * CHUNK_PITCH, HYENA_WIDTH), lambda b, i: (b, i, 0)),
                  _const_spec((1, FOURIER_WIDTH)), _const_spec((1, HYENA_WIDTH)),
                  _const_spec((D_MODEL, D_MODEL)), _const_spec((1, D_MODEL)),
                  _const_spec((D_MODEL, D_FF)), _const_spec((D_FF, D_MODEL)),
                  _const_spec((1, D_MODEL))],
        out_specs=pl.BlockSpec((1, TAIL_TILE, D_MODEL), lambda b, i: (b, i, 0)),
        out_shape=jax.ShapeDtypeStruct((BATCH, SEQ, D_MODEL), F32),
        scratch_shapes=[pltpu.VMEM((TAIL_TILE, D_FF), BF16)],
        compiler_params=_params(('parallel', 'parallel')),
        name='out_proj_mlp',
    )(x, yf, yh, row(g_f), row(g_h), wo, row(g_mlp), w1, w2, row(g_final))


def _filter_features():
    r = np.arange(SEQ)
    pos = ((r % CHUNKS) * N2 + r // CHUNKS).astype(np.float64)
    t = pos / (SEQ - 1)
    bands = np.linspace(1e-4, POS_BANDS - 1, POS_BANDS)
    ang = (2.0 * np.pi * pos / SEQ)[:, None] * bands[None, :]
    z = np.concatenate([t[:, None], np.cos(ang), -np.sin(ang)], axis=-1)
    z = np.pad(z, ((0, 0), (0, LANES - POS_EMB_DIM))).astype(np.float32)
    return jnp.asarray(np.concatenate([z[:SEQ // 2], z[SEQ // 2:]], axis=1))


def _pair(w):
    zero = jnp.zeros_like(w)
    return jnp.concatenate([jnp.concatenate([w, zero], axis=-1),
                            jnp.concatenate([zero, w], axis=-1)], axis=-2)


def kernel(x, g_mix, w_in, conv_w, conv_b, filt_w0, filt_b0, filt_w_inner, filt_b_inner, filt_freq,
           filt_w_out, long_d, g_fourier, g_hyena, w_out, g_mlp, w_fc1, w_fc2, g_final):
    tb = _tables()
    bf = lambda v: v.astype(BF16)

    w0 = jnp.pad(filt_w0[0], ((0, LANES - POS_EMB_DIM), (0, 0)))
    twice = lambda v: jnp.concatenate([v, v], axis=-1)
    deltas = jnp.linspace(MIN_DECAY, MAX_DECAY, HYENA_WIDTH, dtype=F32).reshape(1, HYENA_WIDTH)
    kf = _filters(_filter_features(), _pair(w0), twice(filt_b0[0].reshape(1, -1)),
                  _pair(filt_w_inner[0]),
                  twice(filt_b_inner[0].reshape(FILTER_INNER_LAYERS, 1, FILTER_HIDDEN)),
                  twice(filt_freq[0].reshape(1, -1)), filt_w_out[0], deltas, long_d[0],
                  bf(tb['m1k']), bf(tb['m2']))

    uf, hy = _inproj(x, g_mix[0], bf(w_in[0]))
    yf = _fourier(uf, bf(tb['chan']), bf(tb['m1f']), bf(tb['m2r']))
    nblk = HYENA_WIDTH // LANES
    fft = (bf(tb['m1t']), bf(tb['m2']), bf(tb['m2i']), bf(tb['m1it']))
    z1 = _longconv(hy, 2 * nblk, True, hy, 0, kf, 0, conv_w[0], conv_b[0], *fft)
    z2 = _longconv(z1, 0, False, hy, nblk, kf, 1, conv_w[0], conv_b[0], *fft)

    return _tail(x, yf, z2, g_fourier[0], g_hyena[0], bf(w_out[0]), g_mlp[0],
                 bf(w_fc1[0]), bf(w_fc2[0]), g_final)
```

```python
import functools
import math

import numpy as np
import jax
import jax.numpy as jnp
from jax import lax
from jax.experimental import pallas as pl
from jax.experimental.pallas import tpu as pltpu

F32 = jnp.float32
BF16 = jnp.bfloat16

D_MODEL = 1024
BATCH = 8
SEQ = 4096
FOURIER_WIDTH = 512
FOURIER_GROUP_DIM = 64
HYENA_WIDTH = 512
HYENA_ORDER = 2
POS_BANDS = 16
POS_EMB_DIM = 1 + 2 * POS_BANDS
FILTER_HIDDEN = 64
FILTER_INNER_LAYERS = 2
D_FF = 4 * D_MODEL
DECAY_TARGET = 1e-2
MAX_DECAY = math.log(DECAY_TARGET) / 0.3
MIN_DECAY = math.log(DECAY_TARGET) / 1.5
EPS = 1e-5

LANES = 128
N2 = 128
CHUNKS = SEQ // N2
CHUNK_PITCH = N2 + 8
SEQ_ROWS = CHUNKS * CHUNK_PITCH
N1F = SEQ // N2
N1C = 2 * SEQ // N2
ROWSET = 2 * N1C + 8
FROWSET = 2 * N1F + 8
FPITCH = N1F + 8
FROWS = N2 * FPITCH
IN_COLS = FOURIER_WIDTH + 3 * HYENA_WIDTH
VMEM_LIMIT = 60 * 1024 * 1024
STAGE2_UNROLL = 32
CONV_UNROLL = 64


def _cs(rows, cols, n):
    m = (np.outer(np.arange(rows), np.arange(cols)) % n).astype(np.float64)
    ang = 2.0 * np.pi * m / n
    return np.cos(ang), np.sin(ang)


def _tables():
    t = {}
    c, s = _cs(FOURIER_GROUP_DIM, FOURIER_GROUP_DIM, FOURIER_GROUP_DIM)
    eye = np.eye(LANES // FOURIER_GROUP_DIM)
    t['chan'] = (np.concatenate([np.kron(eye, c), -np.kron(eye, s)], axis=1)
                 / math.sqrt(SEQ * FOURIER_GROUP_DIM))
    m = (np.arange(N1F)[None, :, None] * (N2 * np.arange(N1F)[None, None, :]
                                          + np.arange(N2)[:, None, None])) % SEQ
    ang = 2.0 * np.pi * m.astype(np.float64) / SEQ
    c, s = np.cos(ang), np.sin(ang)
    t['m1f'] = np.concatenate([np.concatenate([c, s], axis=2), np.concatenate([-s, c], axis=2)], axis=1)
    m = (np.arange(N1C)[None, :, None] * (N2 * np.arange(CHUNKS)[None, None, :]
                                          + np.arange(N2)[:, None, None])) % (2 * SEQ)
    ang = 2.0 * np.pi * m.astype(np.float64) / (2 * SEQ)
    c, s = np.cos(ang), np.sin(ang)
    t['m1t'] = np.concatenate([np.concatenate([c, s], axis=2), np.concatenate([-s, c], axis=2)], axis=1)
    c, s = c.transpose(0, 2, 1), s.transpose(0, 2, 1)
    t['m1it'] = np.concatenate([np.concatenate([c, -s], axis=2), np.concatenate([s, c], axis=2)], axis=1)
    order = np.array(list(range(CHUNKS)) + list(range(N1C - 1, CHUNKS - 1, -1)))
    m = (np.arange(N1C)[None, :, None] * (N2 * order[None, None, :]
                                          + np.arange(N2)[:, None, None])) % (2 * SEQ)
    ang = 2.0 * np.pi * m.astype(np.float64) / (2 * SEQ)
    t['m1k'] = np.concatenate([np.cos(ang), -np.sin(ang)], axis=1)
    c, s = _cs(N2, N2, N2)
    t['m2'] = np.block([[c, s], [-s, c]])
    t['m2i'] = np.block([[c, -s], [s, c]])
    t['m2r'] = np.concatenate([c, s], axis=1)
    return {k: jnp.asarray(v.astype(np.float32)) for k, v in t.items()}


def _const_spec(shape):
    nd = len(shape)
    return pl.BlockSpec(shape, lambda *_: (0,) * nd, pipeline_mode=pl.Buffered(1))


def _params(sem):
    return pltpu.CompilerParams(dimension_semantics=sem, vmem_limit_bytes=VMEM_LIMIT)


ROW_TILE = 1024
TILE_CHUNKS = ROW_TILE // N2
ROW_TILES = SEQ // ROW_TILE
FCOLS = FOURIER_WIDTH
HCOLS = 3 * HYENA_WIDTH
SUBLANES = 8


def _rmsnorm(x, g):
    return x * lax.rsqrt(jnp.mean(x * x, axis=-1, keepdims=True) + EPS) * g


def _inproj_kernel(x_ref, g_ref, w_ref, of_ref, o_ref):
    h = _rmsnorm(x_ref[0], g_ref[...]).astype(BF16)
    ph = jnp.dot(h, w_ref[:, FCOLS:], preferred_element_type=F32)
    for j in range(TILE_CHUNKS):
        o_ref[0, j * CHUNK_PITCH:j * CHUNK_PITCH + N2, :] = ph[j * N2:(j + 1) * N2]
        o_ref[0, j * CHUNK_PITCH + N2:(j + 1) * CHUNK_PITCH, :] = jnp.zeros(
            (CHUNK_PITCH - N2, HCOLS), F32)
    of_ref[0] = jnp.dot(h, w_ref[:, :FCOLS], preferred_element_type=F32).astype(BF16)


def _inproj(x, g, w):
    return pl.pallas_call(
        _inproj_kernel,
        grid=(BATCH, ROW_TILES),
        in_specs=[pl.BlockSpec((1, ROW_TILE, D_MODEL), lambda b, i: (b, i, 0)),
                  _const_spec((1, D_MODEL)), _const_spec((D_MODEL, IN_COLS))],
        out_specs=[pl.BlockSpec((1, ROW_TILE, FCOLS), lambda b, i: (b, i, 0)),
                   pl.BlockSpec((1, TILE_CHUNKS * CHUNK_PITCH, HCOLS), lambda b, i: (b, i, 0))],
        out_shape=[jax.ShapeDtypeStruct((BATCH, SEQ, FCOLS), BF16),
                   jax.ShapeDtypeStruct((BATCH, SEQ_ROWS, HCOLS), F32)],
        compiler_params=_params(('parallel', 'parallel')),
        name='in_proj',
    )(x, g.reshape(1, D_MODEL), w)


def _rows(ref, b, n2):
    return ref[b, pl.ds(n2, CHUNKS, stride=CHUNK_PITCH), :]


def _pair_loop(body):
    lax.fori_loop(0, N2 // 2, lambda j, c: (body(j), c)[1], 0, unroll=CONV_UNROLL)
    body(0)
    body(N2 // 2 - 1)


def _shortconv_pair(ref, b, j, w_ref, b_ref):
    n2 = 2 * j
    row = lax.broadcasted_iota(jnp.int32, (CHUNKS, LANES), 0)
    c0 = _rows(ref, b, n2)
    c1 = _rows(ref, b, n2 + 1)
    if isinstance(j, int) and j == 0:
        before = jnp.where(row == 0, 0.0, pltpu.roll(_rows(ref, b, N2 - 1), 1, axis=0))
    elif isinstance(j, int):
        before = _rows(ref, b, n2 - 1)
    else:
        before = _rows(ref, b, jnp.maximum(n2 - 1, 0))
    if isinstance(j, int) and j == N2 // 2 - 1:
        after = jnp.where(row == CHUNKS - 1, 0.0, pltpu.roll(_rows(ref, b, 0), CHUNKS - 1, axis=0))
    else:
        after = _rows(ref, b, n2 + 2)
    w0 = w_ref[0:1, :]
    w1 = w_ref[1:2, :]
    w2 = w_ref[2:3, :]
    bias = b_ref[...]
    return (before * w0 + c0 * w1 + c1 * w2 + bias,
            c0 * w0 + c1 * w1 + after * w2 + bias)


CHAN_CHUNKS = 4


def _fourier_kernel(u_ref, chan_ref, m1_ref, m2_ref, o_ref, z_ref, a_ref):
    def chan(i, carry):
        u = u_ref[0, pl.ds(pl.multiple_of(i * CHAN_CHUNKS * N2, CHAN_CHUNKS * N2), CHAN_CHUNKS * N2), :]
        z = jnp.dot(u, chan_ref[...], preferred_element_type=F32)
        for q in range(CHAN_CHUNKS):
            r0 = pl.multiple_of((CHAN_CHUNKS * i + q) * CHUNK_PITCH, SUBLANES)
            z_ref[0, pl.ds(r0, N2), :] = z[q * N2:(q + 1) * N2, :LANES]
            z_ref[1, pl.ds(r0, N2), :] = z[q * N2:(q + 1) * N2, LANES:]
        return carry

    lax.fori_loop(0, CHUNKS // CHAN_CHUNKS, chan, 0, unroll=True)

    def stage1(j, carry):
        for h in range(2):
            n2 = 2 * j + h
            x = jnp.concatenate([_rows(z_ref, 0, n2), _rows(z_ref, 1, n2)], axis=0).astype(BF16)
            a = jnp.dot(m1_ref[n2], x, preferred_element_type=F32)
            a_ref[pl.ds(pl.multiple_of(n2 * FROWSET, SUBLANES), 2 * N1F), :] = a
        return carry

    lax.fori_loop(0, N2 // 2, stage1, 0, unroll=CONV_UNROLL)

    def stage2(j, carry):
        k1 = 2 * j
        col = lambda r: a_ref[pl.ds(r, N2, stride=FROWSET), :]
        x = jnp.concatenate([jnp.concatenate([col(k1 + h), col(N1F + k1 + h)], axis=0)
                             for h in range(2)], axis=1)
        y = jnp.dot(m2_ref[...], x.astype(BF16), preferred_element_type=F32)
        o_ref[0, pl.ds(k1, N2, stride=FPITCH), :] = y[:, :LANES]
        o_ref[0, pl.ds(k1 + 1, N2, stride=FPITCH), :] = y[:, LANES:]
        return carry

    lax.fori_loop(0, N1F // 2, stage2, 0, unroll=STAGE2_UNROLL)
    for i in range(FPITCH - N1F):
        o_ref[0, pl.ds(N1F + i, N2, stride=FPITCH), :] = jnp.zeros((N2, LANES), F32)


def _fourier(p, chan, m1, m2r):
    nblk = FOURIER_WIDTH // LANES
    consts = (chan, m1, m2r)
    return pl.pallas_call(
        _fourier_kernel,
        grid=(BATCH, nblk),
        in_specs=[pl.BlockSpec((1, SEQ, LANES), lambda b, c: (b, 0, c))]
        + [_const_spec(c.shape) for c in consts],
        out_specs=pl.BlockSpec((1, FROWS, LANES), lambda b, c: (b, 0, c)),
        out_shape=jax.ShapeDtypeStruct((BATCH, FROWS, FOURIER_WIDTH), F32),
        scratch_shapes=[pltpu.VMEM((2, SEQ_ROWS, LANES), F32),
                        pltpu.VMEM((N2 * FROWSET, LANES), F32)],
        compiler_params=_params(('parallel', 'parallel')),
        name='fourier_fft',
    )(p, *consts)


FILT_ROWS = 512


def _split(a):
    hi = a.astype(BF16)
    return hi, (a - hi.astype(F32)).astype(BF16)


def _dot3(a, b_hi, b_lo):
    a_hi, a_lo = _split(a)
    dot = lambda x, y: jnp.dot(x, y, preferred_element_type=F32)
    return dot(a_hi, b_hi) + (dot(a_lo, b_hi) + dot(a_hi, b_lo))


def _filter_kernel(feat_ref, w0_ref, b0_ref, wi_ref, bi_ref, fr_ref, wf_ref, wb_ref, dl_ref, sk_ref,
                   m1_ref, m2_ref, o_ref, h_ref, k_ref, a_ref):
    half = SEQ // 2
    nchunk = half // FILT_ROWS
    blocks = FILT_ROWS // CHUNKS

    @pl.when(pl.program_id(0) == 0)
    def _():
        fr = fr_ref[...]
        w0 = _split(w0_ref[...])
        wi = [_split(wi_ref[j]) for j in range(FILTER_INNER_LAYERS)]

        def mlp(i, carry):
            r0 = pl.multiple_of(i * FILT_ROWS, FILT_ROWS)
            h = jnp.sin(fr * (_dot3(feat_ref[pl.ds(r0, FILT_ROWS), :], *w0) + b0_ref[...]))
            for j in range(FILTER_INNER_LAYERS):
                h = jnp.sin(fr * (_dot3(h, *wi[j]) + bi_ref[j]))
            h_ref[pl.ds(r0, FILT_ROWS), :] = h
            return carry

        lax.fori_loop(0, nchunk, mlp, 0, unroll=True)

    w_both = jnp.concatenate([wf_ref[...], wb_ref[...]], axis=1)
    zero = jnp.zeros_like(w_both)
    w_pair = jnp.concatenate([jnp.concatenate([w_both, zero], axis=1),
                              jnp.concatenate([zero, w_both], axis=1)], axis=0)
    w_pair = _split(w_pair)
    dl = jnp.abs(dl_ref[...])

    def taps(i, carry):
        r0 = pl.multiple_of(i * FILT_ROWS, FILT_ROWS)
        k = _dot3(h_ref[pl.ds(r0, FILT_ROWS), :], *w_pair)
        for s in range(2):
            r = s * half + r0 + lax.broadcasted_iota(jnp.int32, (FILT_ROWS, LANES), 0)
            pos = (r % CHUNKS) * N2 + r // CHUNKS
            decay = jnp.exp(-(pos.astype(F32) / F32(SEQ - 1)) * dl)
            kf = k[:, 2 * s * LANES:(2 * s + 1) * LANES] * decay
            kb = k[:, (2 * s + 1) * LANES:(2 * s + 2) * LANES] * decay
            for q in range(blocks):
                n2 = s * (half // CHUNKS) + i * blocks + q
                k_ref[pl.ds(pl.multiple_of(n2 * N1C, N1C), CHUNKS), :] = kf[q * CHUNKS:(q + 1) * CHUNKS]
                dst = ((N2 - n2) % N2) * N1C + CHUNKS
                k_ref[pl.ds(pl.multiple_of(dst, CHUNKS), CHUNKS), :] = kb[q * CHUNKS:(q + 1) * CHUNKS]
        return carry

    lax.fori_loop(0, nchunk, taps, 0, unroll=True)
    row = lax.broadcasted_iota(jnp.int32, (CHUNKS, LANES), 0)
    kb0 = k_ref[CHUNKS:2 * CHUNKS, :]
    k_ref[0:CHUNKS, :] = k_ref[0:CHUNKS, :] + jnp.where(row == 0, kb0, 0.0)
    k_ref[CHUNKS:2 * CHUNKS, :] = jnp.where(row == CHUNKS - 1, 0.0, pltpu.roll(kb0, CHUNKS - 1, axis=0))

    def stage1(j, ss):
        for h in range(2):
            n2 = 2 * j + h
            x = k_ref[pl.ds(pl.multiple_of(n2 * N1C, N1C), N1C), :]
            a = jnp.dot(m1_ref[n2], x.astype(BF16), preferred_element_type=F32)
            a_ref[pl.ds(pl.multiple_of(n2 * ROWSET, SUBLANES), 2 * N1C), :] = a
            ss = ss + jnp.sum(x * x, axis=0, keepdims=True)
        return ss

    ss = lax.fori_loop(0, N2 // 2, stage1, jnp.zeros((1, LANES), F32), unroll=CONV_UNROLL)
    scale = lax.rsqrt(ss + EPS) * F32(1.0 / (2 * SEQ))
    skip = sk_ref[0] * F32(1.0 / (2 * SEQ))

    def stage2(j, carry):
        k1 = 2 * j
        col = lambda r: a_ref[pl.ds(r, N2, stride=ROWSET), :]
        x = jnp.concatenate([jnp.concatenate([col(k1 + h), col(N1C + k1 + h)], axis=0)
                             for h in range(2)], axis=1)
        y = jnp.dot(m2_ref[...], x.astype(BF16), preferred_element_type=F32)
        o0 = pl.multiple_of(k1 * 2 * N2, 2 * N2)
        for h in range(2):
            yh = y[:, h * LANES:(h + 1) * LANES] * scale
            o_ref[0, pl.ds(o0 + h * 2 * N2, N2), :] = (yh[:N2] + skip).astype(BF16)
            o_ref[0, pl.ds(o0 + h * 2 * N2 + N2, N2), :] = yh[N2:].astype(BF16)
        return carry

    lax.fori_loop(0, N1C // 2, stage2, 0, unroll=STAGE2_UNROLL)


def _filters(feat, w0, b0, wi, bi, fr, w_out, deltas, long_d, m1k, m2):
    nblk = HYENA_WIDTH // LANES
    rows = 2 * SEQ
    return pl.pallas_call(
        _filter_kernel,
        grid=(HYENA_ORDER * nblk,),
        in_specs=[_const_spec(feat.shape), _const_spec(w0.shape), _const_spec(b0.shape),
                  _const_spec(wi.shape), _const_spec(bi.shape), _const_spec(fr.shape),
                  pl.BlockSpec((FILTER_HIDDEN, LANES), lambda g: (0, (g // nblk) * 2 * nblk + g % nblk)),
                  pl.BlockSpec((FILTER_HIDDEN, LANES), lambda g: (0, (g // nblk) * 2 * nblk + nblk + g % nblk)),
                  pl.BlockSpec((1, LANES), lambda g: (0, g % nblk)),
                  pl.BlockSpec((1, 1, LANES), lambda g: (g // nblk, 0, g % nblk)),
                  _const_spec(m1k.shape), _const_spec(m2.shape)],
        out_specs=pl.BlockSpec((1, N1C * 2 * N2, LANES), lambda g: (g // nblk, 0, g % nblk)),
        out_shape=jax.ShapeDtypeStruct((HYENA_ORDER, N1C * 2 * N2, HYENA_WIDTH), BF16),
        scratch_shapes=[pltpu.VMEM((SEQ // 2, 2 * FILTER_HIDDEN), F32),
                        pltpu.VMEM((rows, LANES), F32),
                        pltpu.VMEM((N2 * ROWSET, LANES), F32)],
        compiler_params=_params(('arbitrary',)),
        name='hyena_filters',
    )(feat, w0, b0, wi, bi, fr, w_out, w_out, deltas,
      long_d.reshape(HYENA_ORDER, 1, HYENA_WIDTH), m1k, m2)


def _aligned(v, m):
    return v if isinstance(v, int) else pl.multiple_of(v, m)


def _conv_kernel(*refs, conv_u):
    if conv_u:
        (u_ref, g_ref, kf_ref, uw_ref, ub_ref, gw_ref, gb_ref, m1_ref, m2_ref, m2i_ref, m1i_ref,
         o_ref, a_ref) = refs
    else:
        u_ref, g_ref, kf_ref, gw_ref, gb_ref, m1_ref, m2_ref, m2i_ref, m1i_ref, o_ref, a_ref = refs

    def stage1(j):
        if conv_u:
            u = [_shortconv_pair(u_ref, b, j, uw_ref, ub_ref) for b in range(2)]
        else:
            u = [(_rows(u_ref, b, 2 * j), _rows(u_ref, b, 2 * j + 1)) for b in range(2)]
        for h in range(2):
            n2 = 2 * j + h
            x = jnp.concatenate([u[0][h], u[1][h]], axis=0).astype(BF16)
            a = jnp.dot(m1_ref[n2], x, preferred_element_type=F32)
            a_ref[pl.ds(_aligned(n2 * ROWSET, SUBLANES), 2 * N1C), :] = a

    _pair_loop(stage1)

    def stage2(j, carry):
        k1 = 2 * j
        col = lambda r: a_ref[pl.ds(r, N2, stride=ROWSET), :]
        x = jnp.concatenate([jnp.concatenate([col(k1 + h), col(N1C + k1 + h)], axis=0)
                             for h in range(2)], axis=1)
        s = jnp.dot(m2_ref[...], x.astype(BF16), preferred_element_type=F32)
        ko = pl.multiple_of(k1 * 2 * N2, 2 * N2)
        kf = kf_ref[0, pl.ds(ko, 4 * N2), :].astype(F32)
        kr = jnp.concatenate([kf[0:N2], kf[2 * N2:3 * N2]], axis=1)
        ki = jnp.concatenate([kf[N2:2 * N2], kf[3 * N2:4 * N2]], axis=1)
        xr = s[:N2]
        xi = s[N2:]
        y = jnp.concatenate([xr * kr - xi * ki, xr * ki + xi * kr], axis=0).astype(BF16)
        b = jnp.dot(m2i_ref[...], y, preferred_element_type=F32)
        for h in range(2):
            a_ref[pl.ds(k1 + h, N2, stride=ROWSET), :] = b[:N2, h * LANES:(h + 1) * LANES]
            a_ref[pl.ds(N1C + k1 + h, N2, stride=ROWSET), :] = b[N2:, h * LANES:(h + 1) * LANES]
        return carry

    lax.fori_loop(0, N1C // 2, stage2, 0, unroll=STAGE2_UNROLL)

    def stage3(j):
        g = [_shortconv_pair(g_ref, b, j, gw_ref, gb_ref) for b in range(2)]
        for h in range(2):
            t2 = 2 * j + h
            x = a_ref[pl.ds(_aligned(t2 * ROWSET, SUBLANES), 2 * N1C), :].astype(BF16)
            y = jnp.dot(m1i_ref[t2], x, preferred_element_type=F32)
            for b in range(2):
                o_ref[b, pl.ds(t2, CHUNKS, stride=CHUNK_PITCH), :] = (
                    g[b][h] * y[b * CHUNKS:(b + 1) * CHUNKS])

    _pair_loop(stage3)
    for b in range(2):
        for i in range(CHUNK_PITCH - N2):
            o_ref[b, pl.ds(N2 + i, CHUNKS, stride=CHUNK_PITCH), :] = jnp.zeros((CHUNKS, LANES), F32)


def _longconv(u_arr, u_blk, conv_u, g_arr, g_blk, kf, order, conv_w, conv_b, m1t, m2, m2i, m1it):
    nblk = HYENA_WIDTH // LANES
    blk = (2, SEQ_ROWS, LANES)
    args = [u_arr, g_arr, kf]
    specs = [pl.BlockSpec(blk, lambda c, b: (b, 0, u_blk + c)),
             pl.BlockSpec(blk, lambda c, b: (b, 0, g_blk + c)),
             pl.BlockSpec((1, N1C * 2 * N2, LANES), lambda c, b: (order, 0, c))]
    for blk0 in ([u_blk] if conv_u else []) + [g_blk]:
        args += [conv_w, conv_b.reshape(1, -1)]
        specs += [pl.BlockSpec((3, LANES), lambda c, b, o=blk0: (0, o + c)),
                  pl.BlockSpec((1, LANES), lambda c, b, o=blk0: (0, o + c))]
    consts = (m1t, m2, m2i, m1it)
    return pl.pallas_call(
        functools.partial(_conv_kernel, conv_u=conv_u),
        grid=(nblk, BATCH // 2),
        in_specs=specs + [_const_spec(c.shape) for c in consts],
        out_specs=pl.BlockSpec(blk, lambda c, b: (b, 0, c)),
        out_shape=jax.ShapeDtypeStruct((BATCH, SEQ_ROWS, HYENA_WIDTH), F32),
        scratch_shapes=[pltpu.VMEM((N2 * ROWSET, LANES), F32)],
        compiler_params=_params(('parallel', 'parallel')),
        name=f'hyena_conv{order}',
    )(*args, *consts)


FF_CHUNK = 1024
TAIL_TILE = 1024


def _tail_kernel(x_ref, yf_ref, yh_ref, gf_ref, gh_ref, wo_ref, gm_ref, w1_ref, w2_ref, gl_ref, o_ref,
                 a_ref):
    def norm(v, g):
        return v * lax.rsqrt(jnp.mean(v * v, axis=-1, keepdims=True) + EPS) * g

    yf = jnp.concatenate([yf_ref[0, j * FPITCH:j * FPITCH + N1F, :]
                          for j in range(TAIL_TILE // N1F)], axis=0)
    yh = jnp.concatenate([yh_ref[0, j * CHUNK_PITCH:j * CHUNK_PITCH + N2, :]
                          for j in range(TAIL_TILE // N2)], axis=0)
    y = jnp.concatenate([norm(yf, gf_ref[...]), norm(yh, gh_ref[...])], axis=1).astype(BF16)
    xm = x_ref[0] + jnp.dot(y, wo_ref[...], preferred_element_type=F32)
    hm = norm(xm, gm_ref[...]).astype(BF16)
    for c in range(D_FF // FF_CHUNK):
        a = jnp.dot(hm, w1_ref[:, c * FF_CHUNK:(c + 1) * FF_CHUNK], preferred_element_type=F32)
        a_ref[:, c * FF_CHUNK:(c + 1) * FF_CHUNK] = jnp.square(jnp.maximum(a, 0.0)).astype(BF16)
    acc = xm + jnp.dot(a_ref[...], w2_ref[...], preferred_element_type=F32)
    o_ref[0] = norm(acc, gl_ref[...])


def _tail(x, yf, yh, g_f, g_h, wo, g_mlp, w1, w2, g_final):
    row = lambda v: v.reshape(1, -1)
    return pl.pallas_call(
        _tail_kernel,
        grid=(BATCH, SEQ // TAIL_TILE),
        in_specs=[pl.BlockSpec((1, TAIL_TILE, D_MODEL), lambda b, i: (b, i, 0)),
                  pl.BlockSpec((1, TAIL_TILE // N1F * FPITCH, FOURIER_WIDTH), lambda b, i: (b, i, 0)),
                  pl.BlockSpec((1, TAIL_TILE // N2 * CHUNK_PITCH, HYENA_WIDTH), lambda b, i: (b, i, 0)),
                  _const_spec((1, FOURIER_WIDTH)), _const_spec((1, HYENA_WIDTH)),
                  _const_spec((D_MODEL, D_MODEL)), _const_spec((1, D_MODEL)),
                  _const_spec((D_MODEL, D_FF)), _const_spec((D_FF, D_MODEL)),
                  _const_spec((1, D_MODEL))],
        out_specs=pl.BlockSpec((1, TAIL_TILE, D_MODEL), lambda b, i: (b, i, 0)),
        out_shape=jax.ShapeDtypeStruct((BATCH, SEQ, D_MODEL), F32),
        scratch_shapes=[pltpu.VMEM((TAIL_TILE, D_FF), BF16)],
        compiler_params=_params(('parallel', 'parallel')),
        name='out_proj_mlp',
    )(x, yf, yh, row(g_f), row(g_h), wo, row(g_mlp), w1, w2, row(g_final))


def _filter_features():
    r = np.arange(SEQ)
    pos = ((r % CHUNKS) * N2 + r // CHUNKS).astype(np.float64)
    t = pos / (SEQ - 1)
    bands = np.linspace(1e-4, POS_BANDS - 1, POS_BANDS)
    ang = (2.0 * np.pi * pos / SEQ)[:, None] * bands[None, :]
    z = np.concatenate([t[:, None], np.cos(ang), -np.sin(ang)], axis=-1)
    z = np.pad(z, ((0, 0), (0, LANES - POS_EMB_DIM))).astype(np.float32)
    return jnp.asarray(np.concatenate([z[:SEQ // 2], z[SEQ // 2:]], axis=1))


def _pair(w):
    zero = jnp.zeros_like(w)
    return jnp.concatenate([jnp.concatenate([w, zero], axis=-1),
                            jnp.concatenate([zero, w], axis=-1)], axis=-2)


def kernel(x, g_mix, w_in, conv_w, conv_b, filt_w0, filt_b0, filt_w_inner, filt_b_inner, filt_freq,
           filt_w_out, long_d, g_fourier, g_hyena, w_out, g_mlp, w_fc1, w_fc2, g_final):
    tb = _tables()
    bf = lambda v: v.astype(BF16)

    w0 = jnp.pad(filt_w0[0], ((0, LANES - POS_EMB_DIM), (0, 0)))
    twice = lambda v: jnp.concatenate([v, v], axis=-1)
    deltas = jnp.linspace(MIN_DECAY, MAX_DECAY, HYENA_WIDTH, dtype=F32).reshape(1, HYENA_WIDTH)
    kf = _filters(_filter_features(), _pair(w0), twice(filt_b0[0].reshape(1, -1)),
                  _pair(filt_w_inner[0]),
                  twice(filt_b_inner[0].reshape(FILTER_INNER_LAYERS, 1, FILTER_HIDDEN)),
                  twice(filt_freq[0].reshape(1, -1)), filt_w_out[0], deltas, long_d[0],
                  bf(tb['m1k']), bf(tb['m2']))

    uf, hy = _inproj(x, g_mix[0], bf(w_in[0]))
    yf = _fourier(uf, bf(tb['chan']), bf(tb['m1f']), bf(tb['m2r']))
    nblk = HYENA_WIDTH // LANES
    fft = (bf(tb['m1t']), bf(tb['m2']), bf(tb['m2i']), bf(tb['m1it']))
    z1 = _longconv(hy, 2 * nblk, True, hy, 0, kf, 0, conv_w[0], conv_b[0], *fft)
    z2 = _longconv(z1, 0, False, hy, nblk, kf, 1, conv_w[0], conv_b[0], *fft)

    return _tail(x, yf, z2, g_fourier[0], g_hyena[0], bf(w_out[0]), g_mlp[0],
                 bf(w_fc1[0]), bf(w_fc2[0]), g_final)
```

```python
import functools
import math

import numpy as np
import jax
import jax.numpy as jnp
from jax import lax
from jax.experimental import pallas as pl
from jax.experimental.pallas import tpu as pltpu

F32 = jnp.float32
BF16 = jnp.bfloat16

D_MODEL = 1024
BATCH = 8
SEQ = 4096
FOURIER_WIDTH = 512
FOURIER_GROUP_DIM = 64
HYENA_WIDTH = 512
HYENA_ORDER = 2
POS_BANDS = 16
POS_EMB_DIM = 1 + 2 * POS_BANDS
FILTER_HIDDEN = 64
FILTER_INNER_LAYERS = 2
D_FF = 4 * D_MODEL
DECAY_TARGET = 1e-2
MAX_DECAY = math.log(DECAY_TARGET) / 0.3
MIN_DECAY = math.log(DECAY_TARGET) / 1.5
EPS = 1e-5

LANES = 128
N2 = 128
CHUNKS = SEQ // N2
CHUNK_PITCH = N2 + 8
SEQ_ROWS = CHUNKS * CHUNK_PITCH
N1F = SEQ // N2
N1C = 2 * SEQ // N2
ROWSET = 2 * N1C + 8
FROWSET = 2 * N1F + 8
FPITCH = N1F + 8
FROWS = N2 * FPITCH
IN_COLS = FOURIER_WIDTH + 3 * HYENA_WIDTH
VMEM_LIMIT = 60 * 1024 * 1024
STAGE2_UNROLL = 32
CONV_UNROLL = 64
SPEC_ROWS = 32


def _cs(rows, cols, n):
    m = (np.outer(np.arange(rows), np.arange(cols)) % n).astype(np.float64)
    ang = 2.0 * np.pi * m / n
    return np.cos(ang), np.sin(ang)


def _tables():
    t = {}
    c, s = _cs(FOURIER_GROUP_DIM, FOURIER_GROUP_DIM, FOURIER_GROUP_DIM)
    eye = np.eye(LANES // FOURIER_GROUP_DIM)
    t['chan'] = (np.concatenate([np.kron(eye, c), -np.kron(eye, s)], axis=1)
                 / math.sqrt(SEQ * FOURIER_GROUP_DIM))
    m = (np.arange(N1F)[None, :, None] * (N2 * np.arange(N1F)[None, None, :]
                                          + np.arange(N2)[:, None, None])) % SEQ
    ang = 2.0 * np.pi * m.astype(np.float64) / SEQ
    c, s = np.cos(ang), np.sin(ang)
    t['m1f'] = np.concatenate([np.concatenate([c, s], axis=2), np.concatenate([-s, c], axis=2)], axis=1)
    m = (np.arange(N1C)[None, :, None] * (N2 * np.arange(CHUNKS)[None, None, :]
                                          + np.arange(N2)[:, None, None])) % (2 * SEQ)
    ang = 2.0 * np.pi * m.astype(np.float64) / (2 * SEQ)
    c, s = np.cos(ang), np.sin(ang)
    t['m1t'] = np.concatenate([np.concatenate([c, s], axis=2), np.concatenate([-s, c], axis=2)], axis=1)
    c, s = c.transpose(0, 2, 1), s.transpose(0, 2, 1)
    t['m1it'] = np.concatenate([np.concatenate([c, -s], axis=2), np.concatenate([s, c], axis=2)], axis=1)
    order = np.array(list(range(CHUNKS)) + list(range(N1C - 1, CHUNKS - 1, -1)))
    m = (np.arange(N1C)[None, :, None] * (N2 * order[None, None, :]
                                          + np.arange(N2)[:, None, None])) % (2 * SEQ)
    ang = 2.0 * np.pi * m.astype(np.float64) / (2 * SEQ)
    t['m1k'] = np.concatenate([np.cos(ang), -np.sin(ang)], axis=1)
    c, s = _cs(N2, N2, N2)
    t['m2'] = np.block([[c, s], [-s, c]])
    t['m2i'] = np.block([[c, -s], [s, c]])
    t['m2r'] = np.concatenate([c, s], axis=1)
    return {k: jnp.asarray(v.astype(np.float32)) for k, v in t.items()}


def _const_spec(shape):
    nd = len(shape)
    return pl.BlockSpec(shape, lambda *_: (0,) * nd, pipeline_mode=pl.Buffered(1))


def _params(sem):
    return pltpu.CompilerParams(dimension_semantics=sem, vmem_limit_bytes=VMEM_LIMIT)


ROW_TILE = 1024
TILE_CHUNKS = ROW_TILE // N2
ROW_TILES = SEQ // ROW_TILE
FCOLS = FOURIER_WIDTH
HCOLS = 3 * HYENA_WIDTH
SUBLANES = 8


def _rmsnorm(x, g):
    return x * lax.rsqrt(jnp.mean(x * x, axis=-1, keepdims=True) + EPS) * g


def _inproj_kernel(x_ref, g_ref, w_ref, of_ref, o_ref):
    h = _rmsnorm(x_ref[0], g_ref[...]).astype(BF16)
    ph = jnp.dot(h, w_ref[:, FCOLS:], preferred_element_type=F32)
    for j in range(TILE_CHUNKS):
        o_ref[0, j * CHUNK_PITCH:j * CHUNK_PITCH + N2, :] = ph[j * N2:(j + 1) * N2]
        o_ref[0, j * CHUNK_PITCH + N2:(j + 1) * CHUNK_PITCH, :] = jnp.zeros(
            (CHUNK_PITCH - N2, HCOLS), F32)
    of_ref[0] = jnp.dot(h, w_ref[:, :FCOLS], preferred_element_type=F32).astype(BF16)


def _inproj(x, g, w):
    return pl.pallas_call(
        _inproj_kernel,
        grid=(BATCH, ROW_TILES),
        in_specs=[pl.BlockSpec((1, ROW_TILE, D_MODEL), lambda b, i: (b, i, 0)),
                  _const_spec((1, D_MODEL)), _const_spec((D_MODEL, IN_COLS))],
        out_specs=[pl.BlockSpec((1, ROW_TILE, FCOLS), lambda b, i: (b, i, 0)),
                   pl.BlockSpec((1, TILE_CHUNKS * CHUNK_PITCH, HCOLS), lambda b, i: (b, i, 0))],
        out_shape=[jax.ShapeDtypeStruct((BATCH, SEQ, FCOLS), BF16),
                   jax.ShapeDtypeStruct((BATCH, SEQ_ROWS, HCOLS), F32)],
        compiler_params=_params(('parallel', 'parallel')),
        name='in_proj',
    )(x, g.reshape(1, D_MODEL), w)


def _rows(ref, b, n2):
    return ref[b, pl.ds(n2, CHUNKS, stride=CHUNK_PITCH), :]


def _pair_loop(body):
    lax.fori_loop(0, N2 // 2, lambda j, c: (body(j), c)[1], 0, unroll=CONV_UNROLL)
    body(0)
    body(N2 // 2 - 1)


def _shortconv_pair(ref, b, j, w_ref, b_ref):
    n2 = 2 * j
    row = lax.broadcasted_iota(jnp.int32, (CHUNKS, LANES), 0)
    c0 = _rows(ref, b, n2)
    c1 = _rows(ref, b, n2 + 1)
    if isinstance(j, int) and j == 0:
        before = jnp.where(row == 0, 0.0, pltpu.roll(_rows(ref, b, N2 - 1), 1, axis=0))
    elif isinstance(j, int):
        before = _rows(ref, b, n2 - 1)
    else:
        before = _rows(ref, b, jnp.maximum(n2 - 1, 0))
    if isinstance(j, int) and j == N2 // 2 - 1:
        after = jnp.where(row == CHUNKS - 1, 0.0, pltpu.roll(_rows(ref, b, 0), CHUNKS - 1, axis=0))
    else:
        after = _rows(ref, b, n2 + 2)
    w0 = w_ref[0:1, :]
    w1 = w_ref[1:2, :]
    w2 = w_ref[2:3, :]
    bias = b_ref[...]
    return (before * w0 + c0 * w1 + c1 * w2 + bias,
            c0 * w0 + c1 * w1 + after * w2 + bias)


CHAN_CHUNKS = 4


def _fourier_kernel(u_ref, chan_ref, m1_ref, m2_ref, o_ref, z_ref, a_ref):
    def chan(i, carry):
        u = u_ref[0, pl.ds(pl.multiple_of(i * CHAN_CHUNKS * N2, CHAN_CHUNKS * N2), CHAN_CHUNKS * N2), :]
        z = jnp.dot(u, chan_ref[...], preferred_element_type=F32)
        for q in range(CHAN_CHUNKS):
            r0 = pl.multiple_of((CHAN_CHUNKS * i + q) * CHUNK_PITCH, SUBLANES)
            z_ref[0, pl.ds(r0, N2), :] = z[q * N2:(q + 1) * N2, :LANES]
            z_ref[1, pl.ds(r0, N2), :] = z[q * N2:(q + 1) * N2, LANES:]
        return carry

    lax.fori_loop(0, CHUNKS // CHAN_CHUNKS, chan, 0, unroll=True)

    def stage1(j, carry):
        for h in range(2):
            n2 = 2 * j + h
            x = jnp.concatenate([_rows(z_ref, 0, n2), _rows(z_ref, 1, n2)], axis=0).astype(BF16)
            a = jnp.dot(m1_ref[n2], x, preferred_element_type=F32)
            a_ref[pl.ds(pl.multiple_of(n2 * FROWSET, SUBLANES), 2 * N1F), :] = a
        return carry

    lax.fori_loop(0, N2 // 2, stage1, 0, unroll=CONV_UNROLL)

    def stage2(j, carry):
        k1 = 2 * j
        col = lambda r: a_ref[pl.ds(r, N2, stride=FROWSET), :]
        x = jnp.concatenate([jnp.concatenate([col(k1 + h), col(N1F + k1 + h)], axis=0)
                             for h in range(2)], axis=1)
        y = jnp.dot(m2_ref[...], x.astype(BF16), preferred_element_type=F32)
        o_ref[0, pl.ds(k1, N2, stride=FPITCH), :] = y[:, :LANES]
        o_ref[0, pl.ds(k1 + 1, N2, stride=FPITCH), :] = y[:, LANES:]
        return carry

    lax.fori_loop(0, N1F // 2, stage2, 0, unroll=STAGE2_UNROLL)
    for i in range(FPITCH - N1F):
        o_ref[0, pl.ds(N1F + i, N2, stride=FPITCH), :] = jnp.zeros((N2, LANES), F32)


def _fourier(p, chan, m1, m2r):
    nblk = FOURIER_WIDTH // LANES
    consts = (chan, m1, m2r)
    return pl.pallas_call(
        _fourier_kernel,
        grid=(BATCH, nblk),
        in_specs=[pl.BlockSpec((1, SEQ, LANES), lambda b, c: (b, 0, c))]
        + [_const_spec(c.shape) for c in consts],
        out_specs=pl.BlockSpec((1, FROWS, LANES), lambda b, c: (b, 0, c)),
        out_shape=jax.ShapeDtypeStruct((BATCH, FROWS, FOURIER_WIDTH), F32),
        scratch_shapes=[pltpu.VMEM((2, SEQ_ROWS, LANES), F32),
                        pltpu.VMEM((N2 * FROWSET, LANES), F32)],
        compiler_params=_params(('parallel', 'parallel')),
        name='fourier_fft',
    )(p, *consts)


FILT_ROWS = 512


def _split(a):
    hi = a.astype(BF16)
    return hi, (a - hi.astype(F32)).astype(BF16)


def _dot3(a, b_hi, b_lo):
    a_hi, a_lo = _split(a)
    dot = lambda x, y: jnp.dot(x, y, preferred_element_type=F32)
    return dot(a_hi, b_hi) + (dot(a_lo, b_hi) + dot(a_hi, b_lo))


def _filter_kernel(feat_ref, w0_ref, b0_ref, wi_ref, bi_ref, fr_ref, wf_ref, wb_ref, dl_ref, sk_ref,
                   m1_ref, m2_ref, o_ref, h_ref, k_ref, a_ref):
    half = SEQ // 2
    nchunk = half // FILT_ROWS
    blocks = FILT_ROWS // CHUNKS

    @pl.when(pl.program_id(0) == 0)
    def _():
        fr = fr_ref[...]
        w0 = _split(w0_ref[...])
        wi = [_split(wi_ref[j]) for j in range(FILTER_INNER_LAYERS)]

        def mlp(i, carry):
            r0 = pl.multiple_of(i * FILT_ROWS, FILT_ROWS)
            h = jnp.sin(fr * (_dot3(feat_ref[pl.ds(r0, FILT_ROWS), :], *w0) + b0_ref[...]))
            for j in range(FILTER_INNER_LAYERS):
                h = jnp.sin(fr * (_dot3(h, *wi[j]) + bi_ref[j]))
            h_ref[pl.ds(r0, FILT_ROWS), :] = h
            return carry

        lax.fori_loop(0, nchunk, mlp, 0)

    w_both = jnp.concatenate([wf_ref[...], wb_ref[...]], axis=1)
    zero = jnp.zeros_like(w_both)
    w_pair = jnp.concatenate([jnp.concatenate([w_both, zero], axis=1),
                              jnp.concatenate([zero, w_both], axis=1)], axis=0)
    w_pair = _split(w_pair)
    dl = jnp.abs(dl_ref[...])

    def taps(i, carry):
        r0 = pl.multiple_of(i * FILT_ROWS, FILT_ROWS)
        k = _dot3(h_ref[pl.ds(r0, FILT_ROWS), :], *w_pair)
        for s in range(2):
            r = s * half + r0 + lax.broadcasted_iota(jnp.int32, (FILT_ROWS, LANES), 0)
            pos = (r % CHUNKS) * N2 + r // CHUNKS
            decay = jnp.exp(-(pos.astype(F32) / F32(SEQ - 1)) * dl)
            kf = k[:, 2 * s * LANES:(2 * s + 1) * LANES] * decay
            kb = k[:, (2 * s + 1) * LANES:(2 * s + 2) * LANES] * decay
            for q in range(blocks):
                n2 = s * (half // CHUNKS) + i * blocks + q
                k_ref[pl.ds(pl.multiple_of(n2 * N1C, N1C), CHUNKS), :] = kf[q * CHUNKS:(q + 1) * CHUNKS]
                dst = ((N2 - n2) % N2) * N1C + CHUNKS
                k_ref[pl.ds(pl.multiple_of(dst, CHUNKS), CHUNKS), :] = kb[q * CHUNKS:(q + 1) * CHUNKS]
        return carry

    lax.fori_loop(0, nchunk, taps, 0, unroll=True)
    row = lax.broadcasted_iota(jnp.int32, (CHUNKS, LANES), 0)
    kb0 = k_ref[CHUNKS:2 * CHUNKS, :]
    k_ref[0:CHUNKS, :] = k_ref[0:CHUNKS, :] + jnp.where(row == 0, kb0, 0.0)
    k_ref[CHUNKS:2 * CHUNKS, :] = jnp.where(row == CHUNKS - 1, 0.0, pltpu.roll(kb0, CHUNKS - 1, axis=0))

    def stage1(j, ss):
        for h in range(2):
            n2 = 2 * j + h
            x = k_ref[pl.ds(pl.multiple_of(n2 * N1C, N1C), N1C), :]
            a = jnp.dot(m1_ref[n2], x.astype(BF16), preferred_element_type=F32)
            a_ref[pl.ds(pl.multiple_of(n2 * ROWSET, SUBLANES), 2 * N1C), :] = a
            ss = ss + jnp.sum(x * x, axis=0, keepdims=True)
        return ss

    ss = lax.fori_loop(0, N2 // 2, stage1, jnp.zeros((1, LANES), F32), unroll=CONV_UNROLL)
    scale = lax.rsqrt(ss + EPS) * F32(1.0 / (2 * SEQ))
    skip = sk_ref[0] * F32(1.0 / (2 * SEQ))

    def stage2(j, carry):
        k1 = 2 * j
        col = lambda r: a_ref[pl.ds(r, N2, stride=ROWSET), :]
        x = jnp.concatenate([jnp.concatenate([col(k1 + h), col(N1C + k1 + h)], axis=0)
                             for h in range(2)], axis=1)
        y = jnp.dot(m2_ref[...], x.astype(BF16), preferred_element_type=F32)
        o0 = pl.multiple_of(k1 * 2 * N2, 2 * N2)
        for h in range(2):
            yh = y[:, h * LANES:(h + 1) * LANES] * scale
            o_ref[0, pl.ds(o0 + h * 2 * N2, N2), :] = (yh[:N2] + skip).astype(BF16)
            o_ref[0, pl.ds(o0 + h * 2 * N2 + N2, N2), :] = yh[N2:].astype(BF16)
        return carry

    lax.fori_loop(0, N1C // 2, stage2, 0, unroll=STAGE2_UNROLL)


def _filters(feat, w0, b0, wi, bi, fr, w_out, deltas, long_d, m1k, m2):
    nblk = HYENA_WIDTH // LANES
    rows = 2 * SEQ
    return pl.pallas_call(
        _filter_kernel,
        grid=(HYENA_ORDER * nblk,),
        in_specs=[_const_spec(feat.shape), _const_spec(w0.shape), _const_spec(b0.shape),
                  _const_spec(wi.shape), _const_spec(bi.shape), _const_spec(fr.shape),
                  pl.BlockSpec((FILTER_HIDDEN, LANES), lambda g: (0, (g // nblk) * 2 * nblk + g % nblk)),
                  pl.BlockSpec((FILTER_HIDDEN, LANES), lambda g: (0, (g // nblk) * 2 * nblk + nblk + g % nblk)),
                  pl.BlockSpec((1, LANES), lambda g: (0, g % nblk)),
                  pl.BlockSpec((1, 1, LANES), lambda g: (g // nblk, 0, g % nblk)),
                  _const_spec(m1k.shape), _const_spec(m2.shape)],
        out_specs=pl.BlockSpec((1, N1C * 2 * N2, LANES), lambda g: (g // nblk, 0, g % nblk)),
        out_shape=jax.ShapeDtypeStruct((HYENA_ORDER, N1C * 2 * N2, HYENA_WIDTH), BF16),
        scratch_shapes=[pltpu.VMEM((SEQ // 2, 2 * FILTER_HIDDEN), F32),
                        pltpu.VMEM((rows, LANES), F32),
                        pltpu.VMEM((N2 * ROWSET, LANES), F32)],
        compiler_params=_params(('arbitrary',)),
        name='hyena_filters',
    )(feat, w0, b0, wi, bi, fr, w_out, w_out, deltas,
      long_d.reshape(HYENA_ORDER, 1, HYENA_WIDTH), m1k, m2)


def _aligned(v, m):
    return v if isinstance(v, int) else pl.multiple_of(v, m)


def _conv_kernel(*refs, conv_u):
    if conv_u:
        (u_ref, g_ref, kf_ref, uw_ref, ub_ref, gw_ref, gb_ref, m1_ref, m2_ref, m2i_ref, m1i_ref,
         o_ref, a_ref) = refs
    else:
        u_ref, g_ref, kf_ref, gw_ref, gb_ref, m1_ref, m2_ref, m2i_ref, m1i_ref, o_ref, a_ref = refs

    def stage1(j):
        if conv_u:
            u = [_shortconv_pair(u_ref, b, j, uw_ref, ub_ref) for b in range(2)]
        else:
            u = [(_rows(u_ref, b, 2 * j), _rows(u_ref, b, 2 * j + 1)) for b in range(2)]
        for h in range(2):
            n2 = 2 * j + h
            x = jnp.concatenate([u[0][h], u[1][h]], axis=0).astype(BF16)
            a = jnp.dot(m1_ref[n2], x, preferred_element_type=F32)
            a_ref[pl.ds(_aligned(n2 * ROWSET, SUBLANES), 2 * N1C), :] = a

    _pair_loop(stage1)

    def stage2(j, carry):
        k1 = 2 * j
        col = lambda r: a_ref[pl.ds(r, N2, stride=ROWSET), :]
        x = jnp.concatenate([jnp.concatenate([col(k1 + h), col(N1C + k1 + h)], axis=0)
                             for h in range(2)], axis=1)
        s = jnp.dot(m2_ref[...], x.astype(BF16), preferred_element_type=F32)
        ko = pl.multiple_of(k1 * 2 * N2, 2 * N2)
        yr, yi = [], []
        for r0 in range(0, N2, SPEC_ROWS):
            kch = lambda part, h: kf_ref[0, pl.ds(ko + (2 * h + part) * N2 + r0, SPEC_ROWS), :]
            kr = jnp.concatenate([kch(0, 0), kch(0, 1)], axis=1).astype(F32)
            ki = jnp.concatenate([kch(1, 0), kch(1, 1)], axis=1).astype(F32)
            xr = s[r0:r0 + SPEC_ROWS]
            xi = s[N2 + r0:N2 + r0 + SPEC_ROWS]
            yr.append(xr * kr - xi * ki)
            yi.append(xr * ki + xi * kr)
        y = jnp.concatenate(yr + yi, axis=0).astype(BF16)
        b = jnp.dot(m2i_ref[...], y, preferred_element_type=F32)
        for h in range(2):
            a_ref[pl.ds(k1 + h, N2, stride=ROWSET), :] = b[:N2, h * LANES:(h + 1) * LANES]
            a_ref[pl.ds(N1C + k1 + h, N2, stride=ROWSET), :] = b[N2:, h * LANES:(h + 1) * LANES]
        return carry

    lax.fori_loop(0, N1C // 2, stage2, 0, unroll=STAGE2_UNROLL)

    def stage3(j):
        g = [_shortconv_pair(g_ref, b, j, gw_ref, gb_ref) for b in range(2)]
        for h in range(2):
            t2 = 2 * j + h
            x = a_ref[pl.ds(_aligned(t2 * ROWSET, SUBLANES), 2 * N1C), :].astype(BF16)
            y = jnp.dot(m1i_ref[t2], x, preferred_element_type=F32)
            for b in range(2):
                o_ref[b, pl.ds(t2, CHUNKS, stride=CHUNK_PITCH), :] = (
                    g[b][h] * y[b * CHUNKS:(b + 1) * CHUNKS])

    _pair_loop(stage3)
    for b in range(2):
        for i in range(CHUNK_PITCH - N2):
            o_ref[b, pl.ds(N2 + i, CHUNKS, stride=CHUNK_PITCH), :] = jnp.zeros((CHUNKS, LANES), F32)


def _longconv(u_arr, u_blk, conv_u, g_arr, g_blk, kf, order, conv_w, conv_b, m1t, m2, m2i, m1it):
    nblk = HYENA_WIDTH // LANES
    blk = (2, SEQ_ROWS, LANES)
    args = [u_arr, g_arr, kf]
    specs = [pl.BlockSpec(blk, lambda c, b: (b, 0, u_blk + c)),
             pl.BlockSpec(blk, lambda c, b: (b, 0, g_blk + c)),
             pl.BlockSpec((1, N1C * 2 * N2, LANES), lambda c, b: (order, 0, c))]
    for blk0 in ([u_blk] if conv_u else []) + [g_blk]:
        args += [conv_w, conv_b.reshape(1, -1)]
        specs += [pl.BlockSpec((3, LANES), lambda c, b, o=blk0: (0, o + c)),
                  pl.BlockSpec((1, LANES), lambda c, b, o=blk0: (0, o + c))]
    consts = (m1t, m2, m2i, m1it)
    return pl.pallas_call(
        functools.partial(_conv_kernel, conv_u=conv_u),
        grid=(nblk, BATCH // 2),
        in_specs=specs + [_const_spec(c.shape) for c in consts],
        out_specs=pl.BlockSpec(blk, lambda c, b: (b, 0, c)),
        out_shape=jax.ShapeDtypeStruct((BATCH, SEQ_ROWS, HYENA_WIDTH), F32),
        scratch_shapes=[pltpu.VMEM((N2 * ROWSET, LANES), F32)],
        compiler_params=_params(('parallel', 'parallel')),
        name=f'hyena_conv{order}',
    )(*args, *consts)


FF_CHUNK = 1024
TAIL_TILE = 1024


def _tail_kernel(x_ref, yf_ref, yh_ref, gf_ref, gh_ref, wo_ref, gm_ref, w1_ref, w2_ref, gl_ref, o_ref,
                 a_ref):
    def norm(v, g):
        return v * lax.rsqrt(jnp.mean(v * v, axis=-1, keepdims=True) + EPS) * g

    yf = jnp.concatenate([yf_ref[0, j * FPITCH:j * FPITCH + N1F, :]
                          for j in range(TAIL_TILE // N1F)], axis=0)
    yh = jnp.concatenate([yh_ref[0, j * CHUNK_PITCH:j * CHUNK_PITCH + N2, :]
                          for j in range(TAIL_TILE // N2)], axis=0)
    y = jnp.concatenate([norm(yf, gf_ref[...]), norm(yh, gh_ref[...])], axis=1).astype(BF16)
    xm = x_ref[0] + jnp.dot(y, wo_ref[...], preferred_element_type=F32)
    hm = norm(xm, gm_ref[...]).astype(BF16)
    for c in range(D_FF // FF_CHUNK):
        a = jnp.dot(hm, w1_ref[:, c * FF_CHUNK:(c + 1) * FF_CHUNK], preferred_element_type=F32)
        a_ref[:, c * FF_CHUNK:(c + 1) * FF_CHUNK] = jnp.square(jnp.maximum(a, 0.0)).astype(BF16)
    acc = xm + jnp.dot(a_ref[...], w2_ref[...], preferred_element_type=F32)
    o_ref[0] = norm(acc, gl_ref[...])


def _tail(x, yf, yh, g_f, g_h, wo, g_mlp, w1, w2, g_final):
    row = lambda v: v.reshape(1, -1)
    return pl.pallas_call(
        _tail_kernel,
        grid=(BATCH, SEQ // TAIL_TILE),
        in_specs=[pl.BlockSpec((1, TAIL_TILE, D_MODEL), lambda b, i: (b, i, 0)),
                  pl.BlockSpec((1, TAIL_TILE // N1F * FPITCH, FOURIER_WIDTH), lambda b, i: (b, i, 0)),
                  pl.BlockSpec((1, TAIL_TILE // N2 * CHUNK_PITCH, HYENA_WIDTH), lambda b, i: (b, i, 0)),
                  _const_spec((1, FOURIER_WIDTH)), _const_spec((1, HYENA_WIDTH)),
                  _const_spec((D_MODEL, D_MODEL)), _const_spec((1, D_MODEL)),
                  _const_spec((D_MODEL, D_FF)), _const_spec((D_FF, D_MODEL)),
                  _const_spec((1, D_MODEL))],
        out_specs=pl.BlockSpec((1, TAIL_TILE, D_MODEL), lambda b, i: (b, i, 0)),
        out_shape=jax.ShapeDtypeStruct((BATCH, SEQ, D_MODEL), F32),
        scratch_shapes=[pltpu.VMEM((TAIL_TILE, D_FF), BF16)],
        compiler_params=_params(('parallel', 'parallel')),
        name='out_proj_mlp',
    )(x, yf, yh, row(g_f), row(g_h), wo, row(g_mlp), w1, w2, row(g_final))


def _filter_features():
    r = np.arange(SEQ)
    pos = ((r % CHUNKS) * N2 + r // CHUNKS).astype(np.float64)
    t = pos / (SEQ - 1)
    bands = np.linspace(1e-4, POS_BANDS - 1, POS_BANDS)
    ang = (2.0 * np.pi * pos / SEQ)[:, None] * bands[None, :]
    z = np.concatenate([t[:, None], np.cos(ang), -np.sin(ang)], axis=-1)
    z = np.pad(z, ((0, 0), (0, LANES - POS_EMB_DIM))).astype(np.float32)
    return jnp.asarray(np.concatenate([z[:SEQ // 2], z[SEQ // 2:]], axis=1))


def _pair(w):
    zero = jnp.zeros_like(w)
    return jnp.concatenate([jnp.concatenate([w, zero], axis=-1),
                            jnp.concatenate([zero, w], axis=-1)], axis=-2)


def kernel(x, g_mix, w_in, conv_w, conv_b, filt_w0, filt_b0, filt_w_inner, filt_b_inner, filt_freq,
           filt_w_out, long_d, g_fourier, g_hyena, w_out, g_mlp, w_fc1, w_fc2, g_final):
    tb = _tables()
    bf = lambda v: v.astype(BF16)

    w0 = jnp.pad(filt_w0[0], ((0, LANES - POS_EMB_DIM), (0, 0)))
    twice = lambda v: jnp.concatenate([v, v], axis=-1)
    deltas = jnp.linspace(MIN_DECAY, MAX_DECAY, HYENA_WIDTH, dtype=F32).reshape(1, HYENA_WIDTH)
    kf = _filters(_filter_features(), _pair(w0), twice(filt_b0[0].reshape(1, -1)),
                  _pair(filt_w_inner[0]),
                  twice(filt_b_inner[0].reshape(FILTER_INNER_LAYERS, 1, FILTER_HIDDEN)),
                  twice(filt_freq[0].reshape(1, -1)), filt_w_out[0], deltas, long_d[0],
                  bf(tb['m1k']), bf(tb['m2']))

    uf, hy = _inproj(x, g_mix[0], bf(w_in[0]))
    yf = _fourier(uf, bf(tb['chan']), bf(tb['m1f']), bf(tb['m2r']))
    nblk = HYENA_WIDTH // LANES
    fft = (bf(tb['m1t']), bf(tb['m2']), bf(tb['m2i']), bf(tb['m1it']))
    z1 = _longconv(hy, 2 * nblk, True, hy, 0, kf, 0, conv_w[0], conv_b[0], *fft)
    z2 = _longconv(z1, 0, False, hy, nblk, kf, 1, conv_w[0], conv_b[0], *fft)

    return _tail(x, yf, z2, g_fourier[0], g_hyena[0], bf(w_out[0]), g_mlp[0],
                 bf(w_fc1[0]), bf(w_fc2[0]), g_final)
```
